```python
import jax, jax.numpy as jnp
from jax import lax
import numpy as np

D_MODEL = 1024
BATCH = 32
SEQ = 2048
DEPTH = 1

GRID_W = 64
CTX_LEN = 256

FOURIER_GROUPS = 4
FOURIER_GROUP_DIM = 128
FOURIER_WIDTH = FOURIER_GROUPS * FOURIER_GROUP_DIM
LRU_WIDTH = D_MODEL
LRU_HEADS = 8
LRU_HEAD_DIM = LRU_WIDTH // LRU_HEADS
CONV_WIDTH = 4
CONV_PAD_LEFT = 1
LRU_C = 8.0
IN_WIDTH = FOURIER_WIDTH + 2 * LRU_WIDTH + 2 * D_MODEL
PEER_HEADS = 8
PEER_KEY_DIM = 256
PEER_HALF_DIM = PEER_KEY_DIM // 2
N_KEYS = 128
N_EXPERTS = N_KEYS * N_KEYS
PEER_TOPK = 16
TOKEN_BLOCK = 128
NORM_EPS = 1e-6
POS_BASE = 10000.0

kernel_name = "hybrid_fourier_rglru_peer_dit_block"


def rmsnorm(x, g):
    xf = x.astype(jnp.float32)
    y = xf * lax.rsqrt(jnp.mean(xf * xf, axis=-1, keepdims=True) + NORM_EPS)
    return y.astype(x.dtype) * g


def modulate(h, shift, scale):
    return h * (1.0 + scale) + shift


def adaln(cond, w_mod, b_mod):
    return jnp.split(jax.nn.silu(cond) @ w_mod + b_mod, 6, axis=-1)


def sincos_2d(num_tokens, dim):
    rows = num_tokens // GRID_W
    row = jnp.repeat(jnp.arange(rows, dtype=jnp.float32), GRID_W)
    col = jnp.tile(jnp.arange(GRID_W, dtype=jnp.float32), rows)
    nf = dim // 4
    omega = 1.0 / (POS_BASE ** (jnp.arange(nf, dtype=jnp.float32) / nf))
    er = row[:, None] * omega[None]
    ec = col[:, None] * omega[None]
    return jnp.concatenate([jnp.sin(er), jnp.cos(er), jnp.sin(ec), jnp.cos(ec)], axis=-1)


def fourier_mix(z):
    B, T, _ = z.shape
    zg = z.astype(jnp.float32).reshape(B, T, FOURIER_GROUPS, FOURIER_GROUP_DIM)
    y = jnp.fft.fft2(zg, axes=(1, 3), norm="ortho").real
    return y.reshape(B, T, FOURIER_WIDTH).astype(z.dtype)


def centred_dwconv(z, w, b):
    T = z.shape[1]
    zp = jnp.pad(z, ((0, 0), (CONV_PAD_LEFT, CONV_WIDTH - 1 - CONV_PAD_LEFT), (0, 0)))
    y = b
    for k in range(CONV_WIDTH):
        y = y + zp[:, k:k + T] * w[k]
    return y


def block_diag(z, w):
    B, T, _ = z.shape
    zh = z.reshape(B, T, LRU_HEADS, LRU_HEAD_DIM)
    return jnp.einsum('bthi,hij->bthj', zh, w).reshape(B, T, LRU_WIDTH)


def linear_scan(a, b, h0, reverse):
    def step(h, ab):
        a_t, b_t = ab
        h = a_t * h + b_t
        return h, h
    h_final, hs = lax.scan(step, h0, (jnp.swapaxes(a, 0, 1), jnp.swapaxes(b, 0, 1)), reverse=reverse)
    return jnp.swapaxes(hs, 0, 1), h_final


def rglru_dir(xc, wa, ba, wx, bx, lam, h0, reverse):
    r = jax.nn.sigmoid((block_diag(xc, wa) + ba).astype(jnp.float32))
    i = jax.nn.sigmoid((block_diag(xc, wx) + bx).astype(jnp.float32))
    log_a = -LRU_C * r * jax.nn.softplus(-lam.astype(jnp.float32))
    a = jnp.exp(log_a)
    b = jnp.sqrt(-jnp.expm1(2.0 * log_a)) * (i * xc.astype(jnp.float32))
    return linear_scan(a, b, h0, reverse)


def rglru_bidir(xc, lp, h0_f, h0_b):
    hf, hTf = rglru_dir(xc, lp['lru_wa'][0], lp['lru_ba'][0], lp['lru_wx'][0], lp['lru_bx'][0],
                        lp['lru_lambda'][0], h0_f, False)
    hb, hTb = rglru_dir(xc, lp['lru_wa'][1], lp['lru_ba'][1], lp['lru_wx'][1], lp['lru_bx'][1],
                        lp['lru_lambda'][1], h0_b, True)
    return hf, hb, hTf, hTb


def mixer(h, lp, h0_f, h0_b):
    F, R, D = FOURIER_WIDTH, LRU_WIDTH, D_MODEL
    z = h @ lp['w_in']
    zf, zx, zy, zga, zgb = jnp.split(z, [F, F + R, F + 2 * R, F + 2 * R + D], axis=-1)
    y_a = fourier_mix(zf) @ lp['w_fourier']
    xc = centred_dwconv(zx, lp['conv_w'], lp['conv_b'])
    hf, hb, hTf, hTb = rglru_bidir(xc, lp, h0_f, h0_b)
    y_b = ((hf + hb).astype(h.dtype) * jax.nn.gelu(zy)) @ lp['w_lru_out']
    merged = jax.nn.sigmoid(zga) * y_a + jax.nn.sigmoid(zgb) * y_b
    return merged @ lp['w_out'] + lp['b_out'], hTf, hTb


def context_scan_states(h_c, lp, h0):
    F, R = FOURIER_WIDTH, LRU_WIDTH
    zx = h_c @ lp['w_in'][:, F:F + R]
    xc = centred_dwconv(zx, lp['conv_w'], lp['conv_b'])
    _, _, hTf, hTb = rglru_bidir(xc, lp, h0, h0)
    return hTf, hTb


def peer_block(xb, wq, keys, u_tab, v_tab):
    T = xb.shape[0]
    q = (xb @ wq).reshape(T, PEER_HEADS, 2, PEER_HALF_DIM).astype(jnp.float32)
    s = jnp.einsum('thpd,hpkd->thpk', q, keys.astype(jnp.float32))
    sv, si = lax.top_k(s, PEER_TOPK)
    cand = (sv[:, :, 0, :, None] + sv[:, :, 1, None, :]).reshape(T, PEER_HEADS, PEER_TOPK * PEER_TOPK)
    cidx = (si[:, :, 0, :, None] * N_KEYS + si[:, :, 1, None, :]).reshape(T, PEER_HEADS, PEER_TOPK * PEER_TOPK)
    top_s, pos = lax.top_k(cand, PEER_TOPK)
    eidx = jnp.take_along_axis(cidx, pos, axis=-1)
    g = jax.nn.softmax(top_s, axis=-1)
    u = jnp.take(u_tab, eidx, axis=0)
    act = jax.nn.gelu(jnp.einsum('thkd,td->thk', u, xb).astype(jnp.float32))
    v = jnp.take(v_tab, eidx, axis=0)
    return jnp.einsum('thk,thkd->td', (g * act).astype(xb.dtype), v)


def peer(h, wq, keys, u_tab, v_tab):
    B, T, D = h.shape
    blocks = h.reshape(-1, TOKEN_BLOCK, D)
    out = lax.map(lambda xb: peer_block(xb, wq, keys, u_tab, v_tab), blocks)
    return out.reshape(B, T, D)


def setup_inputs(seed: int = 0) -> dict:
    key = jax.random.key(seed)
    ks = jax.random.split(key, 26)
    D, F, R, L = D_MODEL, FOURIER_WIDTH, LRU_WIDTH, DEPTH
    nrm = lambda k, shape, s: jax.random.normal(k, shape, jnp.float32) * s
    a0 = jax.random.uniform(ks[16], (L, 2, R), jnp.float32, 0.9, 0.999)
    return {
        'x': nrm(ks[0], (BATCH, SEQ, D), 1.0),
        'c': nrm(ks[1], (BATCH, D), 1.0),
        'ctx': nrm(ks[2], (BATCH, CTX_LEN, D), 1.0),
        'c_ctx': nrm(ks[3], (D,), 1.0),
        'w_mod': nrm(ks[4], (L, D, 6 * D), D ** -0.5),
        'b_mod': nrm(ks[5], (L, 6 * D), 0.01),
        'norm1_g': 1.0 + nrm(ks[6], (L, D), 0.05),
        'norm2_g': 1.0 + nrm(ks[7], (L, D), 0.05),
        'w_in': nrm(ks[8], (L, D, IN_WIDTH), D ** -0.5),
        'w_fourier': nrm(ks[9], (L, F, D), F ** -0.5),
        'conv_w': nrm(ks[10], (L, CONV_WIDTH, R), CONV_WIDTH ** -0.5),
        'conv_b': nrm(ks[11], (L, R), 0.01),
        'lru_wa': nrm(ks[12], (L, 2, LRU_HEADS, LRU_HEAD_DIM, LRU_HEAD_DIM), LRU_HEAD_DIM ** -0.5),
        'lru_ba': nrm(ks[13], (L, 2, R), 0.01),
        'lru_wx': nrm(ks[14], (L, 2, LRU_HEADS, LRU_HEAD_DIM, LRU_HEAD_DIM), LRU_HEAD_DIM ** -0.5),
        'lru_bx': nrm(ks[15], (L, 2, R), 0.01),
        'lru_lambda': jnp.log(a0) - jnp.log1p(-a0),
        'w_lru_out': nrm(ks[17], (L, R, D), R ** -0.5),
        'w_out': nrm(ks[18], (L, D, D), D ** -0.5),
        'b_out': nrm(ks[19], (L, D), 0.01),
        'peer_wq': nrm(ks[20], (L, D, PEER_HEADS * PEER_KEY_DIM), D ** -0.5),
        'peer_keys': nrm(ks[21], (L, PEER_HEADS, 2, N_KEYS, PEER_HALF_DIM), PEER_HALF_DIM ** -0.5),
        'peer_u': nrm(ks[22], (L, N_EXPERTS, D), D ** -0.5),
        'peer_v': nrm(ks[23], (L, N_EXPERTS, D), 1.0),
        'final_g': 1.0 + nrm(ks[24], (D,), 0.05),
    }


def reference(x, c, ctx, c_ctx, w_mod, b_mod, norm1_g, norm2_g, w_in, w_fourier, conv_w, conv_b,
              lru_wa, lru_ba, lru_wx, lru_bx, lru_lambda, w_lru_out, w_out, b_out,
              peer_wq, peer_keys, peer_u, peer_v, final_g):
    B, S, D = x.shape
    x = x + sincos_2d(S, D).astype(x.dtype)[None]
    zeros = jnp.zeros((B, LRU_WIDTH), jnp.float32)
    for layer in range(DEPTH):
        is_last = layer == DEPTH - 1
        lp = {
            'w_in': w_in[layer], 'w_fourier': w_fourier[layer],
            'conv_w': conv_w[layer], 'conv_b': conv_b[layer],
            'lru_wa': lru_wa[layer], 'lru_ba': lru_ba[layer],
            'lru_wx': lru_wx[layer], 'lru_bx': lru_bx[layer], 'lru_lambda': lru_lambda[layer],
            'w_lru_out': w_lru_out[layer], 'w_out': w_out[layer], 'b_out': b_out[layer],
        }
        sh1, sc1, g1, sh2, sc2, g2 = adaln(c, w_mod[layer], b_mod[layer])
        csh1, csc1, cg1, csh2, csc2, cg2 = adaln(c_ctx, w_mod[layer], b_mod[layer])

        h_c = modulate(rmsnorm(ctx, norm1_g[layer]), csh1, csc1)
        if is_last:
            hTf, hTb = context_scan_states(h_c, lp, zeros)
        else:
            ctx_mix, hTf, hTb = mixer(h_c, lp, zeros, zeros)
            ctx = ctx + cg1 * ctx_mix
            h2c = modulate(rmsnorm(ctx, norm2_g[layer]), csh2, csc2)
            ctx = ctx + cg2 * peer(h2c, peer_wq[layer], peer_keys[layer], peer_u[layer], peer_v[layer])

        h = modulate(rmsnorm(x, norm1_g[layer]), sh1[:, None], sc1[:, None])
        mix, _, _ = mixer(h, lp, hTf, hTb)
        x = x + g1[:, None] * mix
        h2 = modulate(rmsnorm(x, norm2_g[layer]), sh2[:, None], sc2[:, None])
        x = x + g2[:, None] * peer(h2, peer_wq[layer], peer_keys[layer], peer_u[layer], peer_v[layer])
    return rmsnorm(x, final_g)
```

```python
import functools
import math

import jax
import jax.numpy as jnp
from jax import lax
from jax.experimental import pallas as pl
from jax.experimental.pallas import tpu as pltpu

F32 = jnp.float32
BF16 = jnp.bfloat16

NORM_EPS = 1e-6
POS_BASE = 10000.0
GRID_W = 64
FOURIER_GROUPS = 4
FOURIER_GROUP_DIM = 128
LRU_HEADS = 8
LRU_HEAD_DIM = 128
CONV_WIDTH = 4
LRU_C = 8.0
PEER_HEADS = 8
PEER_HALF_DIM = 128
N_KEYS = 128
PEER_TOPK = 16

LANES = 128
SUBLANES = 8
BF16_ROWS = 16
VMEM_LIMIT = 56 * 1024 * 1024

_CELLS = [(r1, r2) for r1 in range(PEER_TOPK) for r2 in range(PEER_TOPK)
          if (r1 + 1) * (r2 + 1) <= PEER_TOPK]


def _cparams(sem):
    return pltpu.CompilerParams(dimension_semantics=sem, vmem_limit_bytes=VMEM_LIMIT)


def _const_spec(shape):
    nd = len(shape)
    return pl.BlockSpec(shape, lambda *_: (0,) * nd, pipeline_mode=pl.Buffered(1))


def _gelu(x):
    c = math.sqrt(2.0 / math.pi)
    return x * (0.5 * (1.0 + jnp.tanh(c * (x + 0.044715 * (x * x * x)))))


def _norm_mod(x, g, sh, sc):
    ms = jnp.mean(x * x, axis=-1, keepdims=True)
    y = x * lax.rsqrt(ms + NORM_EPS)
    return (y * g) * (1.0 + sc) + sh


def _adaln_kernel(c_ref, w_ref, b_ref, o_ref):
    c = c_ref[...]
    s = c * jax.nn.sigmoid(c)
    o_ref[...] = jnp.dot(s, w_ref[...], preferred_element_type=F32,
                         precision=lax.Precision.HIGHEST) + b_ref[...]


def _adaln(cc, w_mod, b_mod):
    rp, d = cc.shape
    n = w_mod.shape[1]
    tn = 1536
    return pl.pallas_call(
        _adaln_kernel,
        grid=(n // tn,),
        in_specs=[pl.BlockSpec((rp, d), lambda j: (0, 0)),
                  pl.BlockSpec((d, tn), lambda j: (0, j)),
                  pl.BlockSpec((1, tn), lambda j: (0, j))],
        out_specs=pl.BlockSpec((rp, tn), lambda j: (0, j)),
        out_shape=jax.ShapeDtypeStruct((rp, n), F32),
        compiler_params=_cparams(("parallel",)),
        name="adaln",
    )(cc, w_mod, b_mod)


def _ctx_proj_kernel(x_ref, sh_ref, sc_ref, g_ref, w_ref, o_ref):
    h = _norm_mod(x_ref[...], g_ref[...], sh_ref[...], sc_ref[...])
    o_ref[...] = jnp.dot(h.astype(BF16), w_ref[...], preferred_element_type=F32)


def _ctx_proj(ctx2, mod3, row, g1, w_zx):
    n, d = ctx2.shape
    r = w_zx.shape[1]
    tt = min(512, n)
    return pl.pallas_call(
        _ctx_proj_kernel,
        grid=(n // tt,),
        in_specs=[pl.BlockSpec((tt, d), lambda i: (i, 0)),
                  pl.BlockSpec((None, 1, d), lambda i: (row, 0, 0)),
                  pl.BlockSpec((None, 1, d), lambda i: (row, 0, 1)),
                  pl.BlockSpec((1, d), lambda i: (0, 0)),
                  _const_spec((d, r))],
        out_specs=pl.BlockSpec((tt, r), lambda i: (i, 0)),
        out_shape=jax.ShapeDtypeStruct((n, r), F32),
        compiler_params=_cparams(("parallel",)),
        name="ctx_proj",
    )(ctx2, mod3, mod3, g1, w_zx)


def _scan_kernel(zc_ref, zp_ref, zn_ref, cw_ref, cb_ref, wa_ref, ba_ref, wx_ref, bx_ref,
                 lam_ref, h0_ref, o_ref, a_scr, b_scr, h_scr, *, reverse, nt, ts):
    j = pl.program_id(1)
    jj = (nt - 1 - j) if reverse else j

    @pl.when(j == 0)
    def _():
        h_scr[...] = h0_ref[...]

    cur = zc_ref[...]
    has_prev = (jj > 0).astype(F32)
    has_next = (jj < nt - 1).astype(F32)
    prev_row = zp_ref[SUBLANES - 1:SUBLANES, :] * has_prev
    next0 = zn_ref[0:1, :] * has_next
    next1 = zn_ref[1:2, :] * has_next
    row = lax.broadcasted_iota(jnp.int32, cur.shape, 0)
    zm1 = jnp.where(row == 0, prev_row, pltpu.roll(cur, 1, axis=0))
    zp1 = jnp.where(row == ts - 1, next0, pltpu.roll(cur, ts - 1, axis=0))
    zp2 = pltpu.roll(cur, ts - 2, axis=0)
    zp2 = jnp.where(row == ts - 2, next0, jnp.where(row == ts - 1, next1, zp2))
    xc = cb_ref[...] + zm1 * cw_ref[0:1, :]
    xc = xc + cur * cw_ref[1:2, :]
    xc = xc + zp1 * cw_ref[2:3, :]
    xc = xc + zp2 * cw_ref[3:4, :]

    lam = lam_ref[...]
    softplus_neg = jnp.maximum(-lam, 0.0) + jnp.log1p(jnp.exp(-jnp.abs(lam)))
    for hd in range(LRU_HEADS):
        cols = slice(hd * LRU_HEAD_DIM, (hd + 1) * LRU_HEAD_DIM)
        xh = xc[:, cols]
        xb = xh.astype(BF16)
        r = jax.nn.sigmoid(jnp.dot(xb, wa_ref[hd], preferred_element_type=F32) + ba_ref[:, cols])
        i = jax.nn.sigmoid(jnp.dot(xb, wx_ref[hd], preferred_element_type=F32) + bx_ref[:, cols])
        log_a = (-LRU_C * r) * softplus_neg[:, cols]
        a = jnp.exp(log_a)
        a_scr[:, cols] = a
        b_scr[:, cols] = jnp.sqrt(jnp.tanh(-log_a) * (a * a + 1.0)) * (i * xh)

    def step(k, h):
        t = (ts - 1 - k) if reverse else k
        h = a_scr[pl.ds(t, 1), :] * h + b_scr[pl.ds(t, 1), :]
        o_ref[pl.ds(t, 1), :] = h
        return h

    h_scr[...] = lax.fori_loop(0, ts, step, h_scr[...], unroll=8)


def _scan(zx, conv_w, conv_b, wa, ba, wx, bx, lam, h0, reverse):
    b, s, r = zx.shape
    ts = min(256, s)
    nt = s // ts
    nb8 = s // SUBLANES
    per = ts // SUBLANES

    def jj(j):
        return (nt - 1 - j) if reverse else j

    kern = functools.partial(_scan_kernel, reverse=reverse, nt=nt, ts=ts)
    return pl.pallas_call(
        kern,
        grid=(b, nt),
        in_specs=[pl.BlockSpec((None, ts, r), lambda i, j: (i, jj(j), 0)),
                  pl.BlockSpec((None, SUBLANES, r),
                               lambda i, j: (i, jnp.maximum(jj(j) * per - 1, 0), 0)),
                  pl.BlockSpec((None, SUBLANES, r),
                               lambda i, j: (i, jnp.minimum((jj(j) + 1) * per, nb8 - 1), 0)),
                  _const_spec((CONV_WIDTH, r)),
                  _const_spec((1, r)),
                  _const_spec((LRU_HEADS, LRU_HEAD_DIM, LRU_HEAD_DIM)),
                  _const_spec((1, r)),
                  _const_spec((LRU_HEADS, LRU_HEAD_DIM, LRU_HEAD_DIM)),
                  _const_spec((1, r)),
                  _const_spec((1, r)),
                  pl.BlockSpec((None, 1, r), lambda i, j: (i, 0, 0))],
        out_specs=pl.BlockSpec((None, ts, r), lambda i, j: (i, jj(j), 0)),
        out_shape=jax.ShapeDtypeStruct((b, s, r), F32),
        scratch_shapes=[pltpu.VMEM((ts, r), F32), pltpu.VMEM((ts, r), F32),
                        pltpu.VMEM((1, r), F32)],
        compiler_params=_cparams(("arbitrary", "arbitrary")),
        name="scan_bwd" if reverse else "scan_fwd",
    )(zx, zx, zx, conv_w, conv_b, wa, ba, wx, bx, lam, h0)


def _proj_in_kernel(x_ref, pos_ref, sh_ref, sc_ref, g_ref, w_ref, wc_ref,
                    ab_ref, zx_ref, gy_ref, sga_ref, sgb_ref, *, f, r, d):
    xp = x_ref[...] + pos_ref[...]
    h = _norm_mod(xp, g_ref[...], sh_ref[...], sc_ref[...]).astype(BF16)
    zf = jnp.dot(h, w_ref[:, 0:f], preferred_element_type=F32)
    ab_ref[...] = jnp.dot(zf.astype(BF16), wc_ref[...], preferred_element_type=F32).astype(BF16)
    zx_ref[...] = jnp.dot(h, w_ref[:, f:f + r], preferred_element_type=F32)
    zy = jnp.dot(h, w_ref[:, f + r:f + 2 * r], preferred_element_type=F32)
    gy_ref[...] = _gelu(zy).astype(BF16)
    zga = jnp.dot(h, w_ref[:, f + 2 * r:f + 2 * r + d], preferred_element_type=F32)
    sga_ref[...] = jax.nn.sigmoid(zga).astype(BF16)
    zgb = jnp.dot(h, w_ref[:, f + 2 * r + d:f + 2 * r + 2 * d], preferred_element_type=F32)
    sgb_ref[...] = jax.nn.sigmoid(zgb).astype(BF16)


def _proj_in(x, pos, mod3, g1, w_in, wc, f, r):
    b, s, d = x.shape
    tt = min(512, s)
    nt = s // tt
    kern = functools.partial(_proj_in_kernel, f=f, r=r, d=d)
    blk = lambda w: pl.BlockSpec((None, tt, w), lambda j, i: (i, j, 0))
    return pl.pallas_call(
        kern,
        grid=(nt, b),
        in_specs=[blk(d),
                  pl.BlockSpec((tt, d), lambda j, i: (j, 0)),
                  pl.BlockSpec((None, 1, d), lambda j, i: (i, 0, 0)),
                  pl.BlockSpec((None, 1, d), lambda j, i: (i, 0, 1)),
                  pl.BlockSpec((1, d), lambda j, i: (0, 0)),
                  _const_spec(w_in.shape),
                  _const_spec(wc.shape)],
        out_specs=[blk(2 * f), blk(r), blk(r), blk(d), blk(d)],
        out_shape=[jax.ShapeDtypeStruct((b, s, 2 * f), BF16),
                   jax.ShapeDtypeStruct((b, s, r), F32),
                   jax.ShapeDtypeStruct((b, s, r), BF16),
                   jax.ShapeDtypeStruct((b, s, d), BF16),
                   jax.ShapeDtypeStruct((b, s, d), BF16)],
        compiler_params=_cparams(("parallel", "parallel")),
        name="proj_in",
    )(x, pos, mod3, mod3, g1, w_in, wc)


def _mix_kernel(x_ref, pos_ref, g1m_ref, sh2_ref, sc2_ref, g2n_ref, hf_ref, hb_ref, gy_ref,
                sga_ref, sgb_ref, ab_ref, ct_ref, st_ref, wf_ref, wl_ref, wo_ref, bo_ref,
                x1_ref, h2t_ref, *, f):
    yf = jnp.dot(ct_ref[...], ab_ref[:, 0:f], preferred_element_type=F32)
    yf = yf - jnp.dot(st_ref[...], ab_ref[:, f:2 * f], preferred_element_type=F32)
    ya = jnp.dot(yf.astype(BF16), wf_ref[...], preferred_element_type=F32)
    hs = (hf_ref[...] + hb_ref[...]) * gy_ref[...].astype(F32)
    yb = jnp.dot(hs.astype(BF16), wl_ref[...], preferred_element_type=F32)
    merged = sga_ref[...].astype(F32) * ya + sgb_ref[...].astype(F32) * yb
    mix = jnp.dot(merged.astype(BF16), wo_ref[...], preferred_element_type=F32) + bo_ref[...]
    x1 = (x_ref[...] + pos_ref[...]) + g1m_ref[...] * mix
    x1_ref[...] = x1
    h2 = _norm_mod(x1, g2n_ref[...], sh2_ref[...], sc2_ref[...])
    h2t_ref[...] = h2.T.astype(BF16)


def _mix(x, pos, mod3, g2n, hf, hb, gy, sga, sgb, ab, ct, st, wf, wl, wo, bo, f):
    b, s, d = x.shape
    r = hf.shape[-1]
    tm = min(256, s)
    nt = s // tm
    kern = functools.partial(_mix_kernel, f=f)
    blk = lambda w: pl.BlockSpec((None, tm, w), lambda i, j: (i, j, 0))
    modrow = lambda k: pl.BlockSpec((None, 1, d), lambda i, j: (i, 0, k))
    return pl.pallas_call(
        kern,
        grid=(b, nt),
        in_specs=[blk(d),
                  pl.BlockSpec((tm, d), lambda i, j: (j, 0)),
                  modrow(2), modrow(3), modrow(4),
                  pl.BlockSpec((1, d), lambda i, j: (0, 0)),
                  blk(r), blk(r), blk(r), blk(d), blk(d),
                  pl.BlockSpec((None, s, 2 * f), lambda i, j: (i, 0, 0),
                               pipeline_mode=pl.Buffered(1)),
                  pl.BlockSpec((tm, s), lambda i, j: (j, 0)),
                  pl.BlockSpec((tm, s), lambda i, j: (j, 0)),
                  _const_spec(wf.shape), _const_spec(wl.shape), _const_spec(wo.shape),
                  _const_spec(bo.shape)],
        out_specs=[blk(d),
                   pl.BlockSpec((d, tm), lambda i, j: (0, i * nt + j))],
        out_shape=[jax.ShapeDtypeStruct((b, s, d), F32),
                   jax.ShapeDtypeStruct((d, b * s), BF16)],
        compiler_params=_cparams(("parallel", "arbitrary")),
        name="mix",
    )(x, pos, mod3, mod3, mod3, g2n, hf, hb, gy, sga, sgb, ab, ct, st, wf, wl, wo, bo)


def _top16(s, kiota):
    rank = jnp.full(s.shape, float(PEER_TOPK), F32)
    svals = []
    for r in range(PEER_TOPK):
        m = jnp.max(s, axis=0, keepdims=True)
        first = jnp.min(jnp.where(s == m, kiota, float(N_KEYS)), axis=0, keepdims=True)
        hit = kiota == first
        s = jnp.where(hit, -jnp.inf, s)
        rank = jnp.where(hit, float(r), rank)
        svals.append(m)
    return rank, svals


def _route_kernel(h2t_ref, wqt_ref, keys_ref, rank2_ref, e2_ref, cnt_ref, coef_ref,
                  q_scr, s_scr, rank_scr, sv_scr, c_scr, z_scr, *, tr):
    nh = PEER_HEADS
    q_scr[...] = jnp.dot(wqt_ref[...], h2t_ref[...], preferred_element_type=F32).astype(BF16)

    def scores(hp, _):
        off = pl.multiple_of(hp * PEER_HALF_DIM, PEER_HALF_DIM)
        s_scr[hp] = jnp.dot(keys_ref[hp], q_scr[pl.ds(off, PEER_HALF_DIM), :],
                            preferred_element_type=F32)
        return 0

    lax.fori_loop(0, 2 * nh, scores, 0)

    kiota = lax.broadcasted_iota(jnp.int32, (N_KEYS, LANES), 0).astype(F32)

    def lane_block(lb, _):
        l0 = pl.multiple_of(lb * LANES, LANES)
        lanes = pl.ds(l0, LANES)

        def rank_head(hp, _):
            rank, svals = _top16(s_scr[hp, :, lanes], kiota)
            rank_scr[hp] = rank
            half = jnp.bitwise_and(hp, 1)
            head = jnp.right_shift(hp, 1)
            for r in range(PEER_TOPK):
                sv_scr[half, r, pl.ds(head, 1), :] = svals[r]
            return 0

        lax.fori_loop(0, 2 * nh, rank_head, 0)

        sv1 = [sv_scr[0, r] for r in range(PEER_TOPK)]
        sv2 = [sv_scr[1, r] for r in range(PEER_TOPK)]
        val = {c: sv1[c[0]] + sv2[c[1]] for c in _CELLS}
        before = {c: 0.0 for c in _CELLS}
        for ia, ca in enumerate(_CELLS):
            for cb in _CELLS[ia + 1:]:
                if ca[0] <= cb[0] and ca[1] <= cb[1]:
                    before[cb] = before[cb] + 1.0
                elif cb[0] <= ca[0] and cb[1] <= ca[1]:
                    before[ca] = before[ca] + 1.0
                else:
                    ge = jnp.where(val[ca] >= val[cb], 1.0, 0.0)
                    before[cb] = before[cb] + ge
                    before[ca] = before[ca] + (1.0 - ge)
        ex1 = [jnp.exp(sv1[r] - sv1[0]) for r in range(PEER_TOPK)]
        ex2 = [jnp.exp(sv2[r] - sv2[0]) for r in range(PEER_TOPK)]
        zsum = jnp.zeros((nh, LANES), F32)
        cnt = [jnp.zeros((nh, LANES), F32) for _ in range(PEER_TOPK)]
        for c in _CELLS:
            sel = before[c] < float(PEER_TOPK)
            cnt[c[0]] = cnt[c[0]] + jnp.where(sel, 1.0, 0.0)
            zsum = zsum + jnp.where(sel, ex1[c[0]] * ex2[c[1]], 0.0)
        for r in range(PEER_TOPK):
            c_scr[r] = cnt[r]
        z_scr[0] = 1.0 / zsum
        z_scr[1] = sv1[0]
        z_scr[2] = sv2[0]

        def emit_head(h, _):
            hrow = pl.ds(h, 1)
            rank1 = rank_scr[2 * h]
            cnt1 = jnp.zeros((N_KEYS, LANES), F32)
            for r in range(PEER_TOPK):
                cnt1 = jnp.where(rank1 == float(r), c_scr[r, hrow, :], cnt1)
            cnt_ref[h, :, lanes] = cnt1
            s1 = s_scr[2 * h, :, lanes]
            coef_ref[h, :, lanes] = jnp.exp(s1 - z_scr[1, hrow, :]) * z_scr[0, hrow, :]
            rank2 = rank_scr[2 * h + 1]
            s2 = s_scr[2 * h + 1, :, lanes]
            e2 = jnp.exp(s2 - z_scr[2, hrow, :])
            for k in range(N_KEYS // BF16_ROWS):
                rows = slice(k * BF16_ROWS, (k + 1) * BF16_ROWS)
                rank2_ref[h, k, :, lanes] = rank2[rows].astype(BF16)
                e2_ref[h, k, :, lanes] = e2[rows].astype(BF16)
            return 0

        lax.fori_loop(0, nh, emit_head, 0)
        return 0

    lax.fori_loop(0, tr // LANES, lane_block, 0)


def _route(h2t, wqt, keys):
    d, t = h2t.shape
    nh = PEER_HEADS
    tr = 256
    nk16 = N_KEYS // BF16_ROWS
    kern = functools.partial(_route_kernel, tr=tr)
    bblk = pl.BlockSpec((nh, nk16, BF16_ROWS, tr), lambda i: (0, 0, 0, i))
    fblk = pl.BlockSpec((nh, N_KEYS, tr), lambda i: (0, 0, i))
    return pl.pallas_call(
        kern,
        grid=(t // tr,),
        in_specs=[pl.BlockSpec((d, tr), lambda i: (0, i)),
                  _const_spec(wqt.shape),
                  _const_spec(keys.shape)],
        out_specs=[bblk, bblk, fblk, fblk],
        out_shape=[jax.ShapeDtypeStruct((nh, nk16, BF16_ROWS, t), BF16),
                   jax.ShapeDtypeStruct((nh, nk16, BF16_ROWS, t), BF16),
                   jax.ShapeDtypeStruct((nh, N_KEYS, t), F32),
                   jax.ShapeDtypeStruct((nh, N_KEYS, t), F32)],
        scratch_shapes=[pltpu.VMEM((2 * nh * PEER_HALF_DIM, tr), BF16),
                        pltpu.VMEM((2 * nh, N_KEYS, tr), F32),
                        pltpu.VMEM((2 * nh, N_KEYS, LANES), F32),
                        pltpu.VMEM((2, PEER_TOPK, nh, LANES), F32),
                        pltpu.VMEM((PEER_TOPK, nh, LANES), F32),
                        pltpu.VMEM((3, nh, LANES), F32)],
        compiler_params=_cparams(("parallel",)),
        name="route",
    )(h2t, wqt, keys)


def _peer_kernel(h2t_ref, u_ref, vt_ref, rank2_ref, e2_ref, cnt_ref, coef_ref, x1_ref,
                 g2m_ref, gf_ref, o_ref, acc_ref, wt_ref, *, eb, ne, tm):
    e = pl.program_id(1)
    nk16 = N_KEYS // BF16_ROWS

    @pl.when(e == 0)
    def _():
        acc_ref[...] = jnp.zeros_like(acc_ref)

    st = jnp.dot(u_ref[...], h2t_ref[...], preferred_element_type=F32)
    for k in range(eb // N_KEYS):
        i1 = e * (eb // N_KEYS) + k
        cnt = [jnp.broadcast_to(cnt_ref[h, pl.ds(i1, 1), :], (BF16_ROWS, tm)).astype(BF16)
               for h in range(PEER_HEADS)]
        cf = [jnp.broadcast_to(coef_ref[h, pl.ds(i1, 1), :], (BF16_ROWS, tm)).astype(BF16)
              for h in range(PEER_HEADS)]
        for c in range(nk16):
            m = None
            for h in range(PEER_HEADS):
                term = jnp.where(rank2_ref[h, c] < cnt[h], e2_ref[h, c] * cf[h],
                                 jnp.zeros((), BF16))
                m = term if m is None else m + term
            r0 = k * N_KEYS + c * BF16_ROWS
            act = _gelu(st[r0:r0 + BF16_ROWS, :])
            wt_ref[r0:r0 + BF16_ROWS, :] = (m.astype(F32) * act).astype(BF16)
    acc_ref[...] += jnp.dot(vt_ref[...], wt_ref[...], preferred_element_type=F32)

    @pl.when(e == ne - 1)
    def _():
        x2 = x1_ref[...] + g2m_ref[...] * acc_ref[...].T
        ms = jnp.mean(x2 * x2, axis=-1, keepdims=True)
        o_ref[...] = (x2 * lax.rsqrt(ms + NORM_EPS)) * gf_ref[...]


def _peer(h2t, u, vt, rank2, e2, cnt, coef, x1, mod3, gf):
    b, s, d = x1.shape
    t = b * s
    ne_total = u.shape[0]
    tm = min(512, s)
    eb = 1024
    ne = ne_total // eb
    per_b = s // tm
    nk16 = N_KEYS // BF16_ROWS
    kern = functools.partial(_peer_kernel, eb=eb, ne=ne, tm=tm)
    bblk = pl.BlockSpec((PEER_HEADS, nk16, BF16_ROWS, tm), lambda i, e: (0, 0, 0, i))
    fblk = pl.BlockSpec((PEER_HEADS, N_KEYS, tm), lambda i, e: (0, 0, i))
    xblk = pl.BlockSpec((None, tm, d), lambda i, e: (i // per_b, i % per_b, 0))
    return pl.pallas_call(
        kern,
        grid=(t // tm, ne),
        in_specs=[pl.BlockSpec((d, tm), lambda i, e: (0, i)),
                  pl.BlockSpec((eb, d), lambda i, e: (e, 0)),
                  pl.BlockSpec((d, eb), lambda i, e: (0, e)),
                  bblk, bblk, fblk, fblk,
                  xblk,
                  pl.BlockSpec((None, 1, d), lambda i, e: (i // per_b, 0, 5)),
                  pl.BlockSpec((1, d), lambda i, e: (0, 0))],
        out_specs=xblk,
        out_shape=jax.ShapeDtypeStruct((b, s, d), F32),
        scratch_shapes=[pltpu.VMEM((d, tm), F32), pltpu.VMEM((eb, tm), BF16)],
        compiler_params=_cparams(("parallel", "arbitrary")),
        name="peer",
    )(h2t, u, vt, rank2, e2, cnt, coef, x1, mod3, gf)


def _sincos_2d(num_tokens, dim):
    rows = num_tokens // GRID_W
    row = jnp.repeat(jnp.arange(rows, dtype=F32), GRID_W)
    col = jnp.tile(jnp.arange(GRID_W, dtype=F32), rows)
    nf = dim // 4
    omega = 1.0 / (POS_BASE ** (jnp.arange(nf, dtype=F32) / nf))
    er = row[:, None] * omega[None]
    ec = col[:, None] * omega[None]
    return jnp.concatenate([jnp.sin(er), jnp.cos(er), jnp.sin(ec), jnp.cos(ec)], axis=-1)


def _dft_tables(n):
    k = jnp.arange(n, dtype=jnp.int32)
    ang = ((k[:, None] * k[None, :]) % n).astype(F32) * (2.0 * math.pi / n)
    return jnp.cos(ang), jnp.sin(ang)


def kernel(x, c, ctx, c_ctx, w_mod, b_mod, norm1_g, norm2_g, w_in, w_fourier, conv_w, conv_b,
           lru_wa, lru_ba, lru_wx, lru_bx, lru_lambda, w_lru_out, w_out, b_out,
           peer_wq, peer_keys, peer_u, peer_v, final_g):
    b, s, d = x.shape
    ctx_len = ctx.shape[1]
    f = FOURIER_GROUPS * FOURIER_GROUP_DIM
    r = LRU_HEADS * LRU_HEAD_DIM
    assert w_mod.shape[0] == 1, "single-layer block"
    assert w_in.shape[2] == f + 2 * r + 2 * d

    pos = _sincos_2d(s, d)
    ct, st = _dft_tables(s)
    cc, sc = _dft_tables(FOURIER_GROUP_DIM)
    scale = 1.0 / math.sqrt(s * FOURIER_GROUP_DIM)
    eye = jnp.eye(FOURIER_GROUPS, dtype=F32)
    wc = jnp.concatenate([jnp.kron(eye, cc), jnp.kron(eye, sc)], axis=1) * scale

    rp = -(-(b + 1) // SUBLANES) * SUBLANES
    cond = jnp.concatenate([c, c_ctx[None], jnp.zeros((rp - b - 1, d), F32)], axis=0)
    mod = _adaln(cond, w_mod[0], b_mod)
    mod3 = mod.reshape(rp, 1, 6 * d)

    w_in_b = w_in[0].astype(BF16)
    g1 = norm1_g
    g2 = norm2_g
    wa = lru_wa[0].astype(BF16)
    wx = lru_wx[0].astype(BF16)
    ba, bx, lam = lru_ba[0], lru_bx[0], lru_lambda[0]

    def scans(zx, h0f, h0b):
        hf = _scan(zx, conv_w[0], conv_b, wa[0], ba[0:1], wx[0], bx[0:1], lam[0:1], h0f, False)
        hb = _scan(zx, conv_w[0], conv_b, wa[1], ba[1:2], wx[1], bx[1:2], lam[1:2], h0b, True)
        return hf, hb

    zx_c = _ctx_proj(ctx.reshape(b * ctx_len, d), mod3, b, g1, w_in_b[:, f:f + r])
    zeros = jnp.zeros((b, 1, r), F32)
    hf_c, hb_c = scans(zx_c.reshape(b, ctx_len, r), zeros, zeros)
    htf = hf_c[:, ctx_len - 1:ctx_len, :]
    htb = hb_c[:, 0:1, :]

    ab, zx, gy, sga, sgb = _proj_in(x, pos, mod3, g1, w_in_b, wc.astype(BF16), f, r)
    hf, hb = scans(zx, htf, htb)
    x1, h2t = _mix(x, pos, mod3, g2, hf, hb, gy, sga, sgb, ab,
                   ct.astype(BF16), st.astype(BF16),
                   w_fourier[0].astype(BF16), w_lru_out[0].astype(BF16),
                   w_out[0].astype(BF16), b_out, f)

    wqt = peer_wq[0].T.astype(BF16)
    keys = peer_keys[0].reshape(2 * PEER_HEADS, N_KEYS, PEER_HALF_DIM).astype(BF16)
    rank2, e2, cnt, coef = _route(h2t, wqt, keys)
    u = peer_u[0].astype(BF16)
    vt = peer_v[0].T.astype(BF16)
    return _peer(h2t, u, vt, rank2, e2, cnt, coef, x1, mod3, final_g[None])
```

```python
import functools
import math

import jax
import jax.numpy as jnp
from jax import lax
from jax.experimental import pallas as pl
from jax.experimental.pallas import tpu as pltpu

F32 = jnp.float32
BF16 = jnp.bfloat16

NORM_EPS = 1e-6
POS_BASE = 10000.0
GRID_W = 64
FOURIER_GROUPS = 4
FOURIER_GROUP_DIM = 128
LRU_HEADS = 8
LRU_HEAD_DIM = 128
CONV_WIDTH = 4
LRU_C = 8.0
PEER_HEADS = 8
PEER_HALF_DIM = 128
N_KEYS = 128
PEER_TOPK = 16

LANES = 128
SUBLANES = 8
BF16_ROWS = 16
VMEM_LIMIT = 56 * 1024 * 1024

_CELLS = [(r1, r2) for r1 in range(PEER_TOPK) for r2 in range(PEER_TOPK)
          if (r1 + 1) * (r2 + 1) <= PEER_TOPK]


def _cparams(sem):
    return pltpu.CompilerParams(dimension_semantics=sem, vmem_limit_bytes=VMEM_LIMIT)


def _const_spec(shape):
    nd = len(shape)
    return pl.BlockSpec(shape, lambda *_: (0,) * nd, pipeline_mode=pl.Buffered(1))


def _gelu(x):
    c = math.sqrt(2.0 / math.pi)
    return x * (0.5 * (1.0 + jnp.tanh(c * (x + 0.044715 * (x * x * x)))))


def _gelu_x2(x):
    c = math.sqrt(2.0 / math.pi)
    return x * (1.0 + jnp.tanh(x * (c + (c * 0.044715) * (x * x))))


def _norm_mod(x, g, sh, sc):
    ms = jnp.mean(x * x, axis=-1, keepdims=True)
    y = x * lax.rsqrt(ms + NORM_EPS)
    return (y * g) * (1.0 + sc) + sh


def _adaln_kernel(c_ref, w_ref, b_ref, o_ref):
    c = c_ref[...]
    s = c * jax.nn.sigmoid(c)
    o_ref[...] = jnp.dot(s, w_ref[...], preferred_element_type=F32,
                         precision=lax.Precision.HIGHEST) + b_ref[...]


def _adaln(cc, w_mod, b_mod):
    rp, d = cc.shape
    n = w_mod.shape[1]
    tn = 1536
    return pl.pallas_call(
        _adaln_kernel,
        grid=(n // tn,),
        in_specs=[pl.BlockSpec((rp, d), lambda j: (0, 0)),
                  pl.BlockSpec((d, tn), lambda j: (0, j)),
                  pl.BlockSpec((1, tn), lambda j: (0, j))],
        out_specs=pl.BlockSpec((rp, tn), lambda j: (0, j)),
        out_shape=jax.ShapeDtypeStruct((rp, n), F32),
        compiler_params=_cparams(("parallel",)),
        name="adaln",
    )(cc, w_mod, b_mod)


def _ctx_proj_kernel(x_ref, sh_ref, sc_ref, g_ref, w_ref, o_ref):
    h = _norm_mod(x_ref[...], g_ref[...], sh_ref[...], sc_ref[...])
    o_ref[...] = jnp.dot(h.astype(BF16), w_ref[...], preferred_element_type=F32)


def _ctx_proj(ctx2, mod3, row, g1, w_zx):
    n, d = ctx2.shape
    r = w_zx.shape[1]
    tt = min(512, n)
    return pl.pallas_call(
        _ctx_proj_kernel,
        grid=(n // tt,),
        in_specs=[pl.BlockSpec((tt, d), lambda i: (i, 0)),
                  pl.BlockSpec((None, 1, d), lambda i: (row, 0, 0)),
                  pl.BlockSpec((None, 1, d), lambda i: (row, 0, 1)),
                  pl.BlockSpec((1, d), lambda i: (0, 0)),
                  _const_spec((d, r))],
        out_specs=pl.BlockSpec((tt, r), lambda i: (i, 0)),
        out_shape=jax.ShapeDtypeStruct((n, r), F32),
        compiler_params=_cparams(("parallel",)),
        name="ctx_proj",
    )(ctx2, mod3, mod3, g1, w_zx)


def _scan_kernel(zc_ref, zp_ref, zn_ref, cw_ref, cb_ref, wa_ref, ba_ref, wx_ref, bx_ref,
                 lam_ref, h0_ref, o_ref, a_scr, b_scr, h_scr, *, reverse, nt, ts):
    j = pl.program_id(1)
    jj = (nt - 1 - j) if reverse else j

    @pl.when(j == 0)
    def _():
        h_scr[...] = h0_ref[...]

    cur = zc_ref[...]
    has_prev = (jj > 0).astype(F32)
    has_next = (jj < nt - 1).astype(F32)
    prev_row = zp_ref[SUBLANES - 1:SUBLANES, :] * has_prev
    next0 = zn_ref[0:1, :] * has_next
    next1 = zn_ref[1:2, :] * has_next
    row = lax.broadcasted_iota(jnp.int32, cur.shape, 0)
    zm1 = jnp.where(row == 0, prev_row, pltpu.roll(cur, 1, axis=0))
    zp1 = jnp.where(row == ts - 1, next0, pltpu.roll(cur, ts - 1, axis=0))
    zp2 = pltpu.roll(cur, ts - 2, axis=0)
    zp2 = jnp.where(row == ts - 2, next0, jnp.where(row == ts - 1, next1, zp2))
    xc = cb_ref[...] + zm1 * cw_ref[0:1, :]
    xc = xc + cur * cw_ref[1:2, :]
    xc = xc + zp1 * cw_ref[2:3, :]
    xc = xc + zp2 * cw_ref[3:4, :]

    lam = lam_ref[...]
    softplus_neg = jnp.maximum(-lam, 0.0) + jnp.log1p(jnp.exp(-jnp.abs(lam)))
    for hd in range(LRU_HEADS):
        cols = slice(hd * LRU_HEAD_DIM, (hd + 1) * LRU_HEAD_DIM)
        xh = xc[:, cols]
        xb = xh.astype(BF16)
        r = jax.nn.sigmoid(jnp.dot(xb, wa_ref[hd], preferred_element_type=F32) + ba_ref[:, cols])
        i = jax.nn.sigmoid(jnp.dot(xb, wx_ref[hd], preferred_element_type=F32) + bx_ref[:, cols])
        log_a = (-LRU_C * r) * softplus_neg[:, cols]
        a = jnp.exp(log_a)
        a_scr[:, cols] = a
        b_scr[:, cols] = jnp.sqrt(jnp.tanh(-log_a) * (a * a + 1.0)) * (i * xh)

    def step(k, h):
        t = (ts - 1 - k) if reverse else k
        h = a_scr[pl.ds(t, 1), :] * h + b_scr[pl.ds(t, 1), :]
        o_ref[pl.ds(t, 1), :] = h
        return h

    h_scr[...] = lax.fori_loop(0, ts, step, h_scr[...], unroll=8)


def _scan(zx, conv_w, conv_b, wa, ba, wx, bx, lam, h0, reverse):
    b, s, r = zx.shape
    ts = min(256, s)
    nt = s // ts
    nb8 = s // SUBLANES
    per = ts // SUBLANES

    def jj(j):
        return (nt - 1 - j) if reverse else j

    kern = functools.partial(_scan_kernel, reverse=reverse, nt=nt, ts=ts)
    return pl.pallas_call(
        kern,
        grid=(b, nt),
        in_specs=[pl.BlockSpec((None, ts, r), lambda i, j: (i, jj(j), 0)),
                  pl.BlockSpec((None, SUBLANES, r),
                               lambda i, j: (i, jnp.maximum(jj(j) * per - 1, 0), 0)),
                  pl.BlockSpec((None, SUBLANES, r),
                               lambda i, j: (i, jnp.minimum((jj(j) + 1) * per, nb8 - 1), 0)),
                  _const_spec((CONV_WIDTH, r)),
                  _const_spec((1, r)),
                  _const_spec((LRU_HEADS, LRU_HEAD_DIM, LRU_HEAD_DIM)),
                  _const_spec((1, r)),
                  _const_spec((LRU_HEADS, LRU_HEAD_DIM, LRU_HEAD_DIM)),
                  _const_spec((1, r)),
                  _const_spec((1, r)),
                  pl.BlockSpec((None, 1, r), lambda i, j: (i, 0, 0))],
        out_specs=pl.BlockSpec((None, ts, r), lambda i, j: (i, jj(j), 0)),
        out_shape=jax.ShapeDtypeStruct((b, s, r), F32),
        scratch_shapes=[pltpu.VMEM((ts, r), F32), pltpu.VMEM((ts, r), F32),
                        pltpu.VMEM((1, r), F32)],
        compiler_params=_cparams(("arbitrary", "arbitrary")),
        name="scan_bwd" if reverse else "scan_fwd",
    )(zx, zx, zx, conv_w, conv_b, wa, ba, wx, bx, lam, h0)


def _proj_in_kernel(x_ref, pos_ref, sh_ref, sc_ref, g_ref, w_ref, wc_ref,
                    ab_ref, zx_ref, gy_ref, sga_ref, sgb_ref, *, f, r, d):
    xp = x_ref[...] + pos_ref[...]
    h = _norm_mod(xp, g_ref[...], sh_ref[...], sc_ref[...]).astype(BF16)
    zf = jnp.dot(h, w_ref[:, 0:f], preferred_element_type=F32)
    ab_ref[...] = jnp.dot(zf.astype(BF16), wc_ref[...], preferred_element_type=F32).astype(BF16)
    zx_ref[...] = jnp.dot(h, w_ref[:, f:f + r], preferred_element_type=F32)
    zy = jnp.dot(h, w_ref[:, f + r:f + 2 * r], preferred_element_type=F32)
    gy_ref[...] = _gelu(zy).astype(BF16)
    zga = jnp.dot(h, w_ref[:, f + 2 * r:f + 2 * r + d], preferred_element_type=F32)
    sga_ref[...] = jax.nn.sigmoid(zga).astype(BF16)
    zgb = jnp.dot(h, w_ref[:, f + 2 * r + d:f + 2 * r + 2 * d], preferred_element_type=F32)
    sgb_ref[...] = jax.nn.sigmoid(zgb).astype(BF16)


def _proj_in(x, pos, mod3, g1, w_in, wc, f, r):
    b, s, d = x.shape
    tt = min(512, s)
    nt = s // tt
    kern = functools.partial(_proj_in_kernel, f=f, r=r, d=d)
    blk = lambda w: pl.BlockSpec((None, tt, w), lambda j, i: (i, j, 0))
    return pl.pallas_call(
        kern,
        grid=(nt, b),
        in_specs=[blk(d),
                  pl.BlockSpec((tt, d), lambda j, i: (j, 0)),
                  pl.BlockSpec((None, 1, d), lambda j, i: (i, 0, 0)),
                  pl.BlockSpec((None, 1, d), lambda j, i: (i, 0, 1)),
                  pl.BlockSpec((1, d), lambda j, i: (0, 0)),
                  _const_spec(w_in.shape),
                  _const_spec(wc.shape)],
        out_specs=[blk(2 * f), blk(r), blk(r), blk(d), blk(d)],
        out_shape=[jax.ShapeDtypeStruct((b, s, 2 * f), BF16),
                   jax.ShapeDtypeStruct((b, s, r), F32),
                   jax.ShapeDtypeStruct((b, s, r), BF16),
                   jax.ShapeDtypeStruct((b, s, d), BF16),
                   jax.ShapeDtypeStruct((b, s, d), BF16)],
        compiler_params=_cparams(("parallel", "parallel")),
        name="proj_in",
    )(x, pos, mod3, mod3, g1, w_in, wc)


def _mix_kernel(x_ref, pos_ref, g1m_ref, sh2_ref, sc2_ref, g2n_ref, hf_ref, hb_ref, gy_ref,
                sga_ref, sgb_ref, ab_ref, ct_ref, st_ref, wf_ref, wl_ref, wo_ref, bo_ref,
                x1_ref, h2t_ref, *, f):
    yf = jnp.dot(ct_ref[...], ab_ref[:, 0:f], preferred_element_type=F32)
    yf = yf - jnp.dot(st_ref[...], ab_ref[:, f:2 * f], preferred_element_type=F32)
    ya = jnp.dot(yf.astype(BF16), wf_ref[...], preferred_element_type=F32)
    hs = (hf_ref[...] + hb_ref[...]) * gy_ref[...].astype(F32)
    yb = jnp.dot(hs.astype(BF16), wl_ref[...], preferred_element_type=F32)
    merged = sga_ref[...].astype(F32) * ya + sgb_ref[...].astype(F32) * yb
    mix = jnp.dot(merged.astype(BF16), wo_ref[...], preferred_element_type=F32) + bo_ref[...]
    x1 = (x_ref[...] + pos_ref[...]) + g1m_ref[...] * mix
    x1_ref[...] = x1
    h2 = _norm_mod(x1, g2n_ref[...], sh2_ref[...], sc2_ref[...])
    h2t_ref[...] = h2.T.astype(BF16)


def _mix(x, pos, mod3, g2n, hf, hb, gy, sga, sgb, ab, ct, st, wf, wl, wo, bo, f):
    b, s, d = x.shape
    r = hf.shape[-1]
    tm = min(256, s)
    nt = s // tm
    kern = functools.partial(_mix_kernel, f=f)
    blk = lambda w: pl.BlockSpec((None, tm, w), lambda i, j: (i, j, 0))
    modrow = lambda k: pl.BlockSpec((None, 1, d), lambda i, j: (i, 0, k))
    return pl.pallas_call(
        kern,
        grid=(b, nt),
        in_specs=[blk(d),
                  pl.BlockSpec((tm, d), lambda i, j: (j, 0)),
                  modrow(2), modrow(3), modrow(4),
                  pl.BlockSpec((1, d), lambda i, j: (0, 0)),
                  blk(r), blk(r), blk(r), blk(d), blk(d),
                  pl.BlockSpec((None, s, 2 * f), lambda i, j: (i, 0, 0),
                               pipeline_mode=pl.Buffered(1)),
                  pl.BlockSpec((tm, s), lambda i, j: (j, 0)),
                  pl.BlockSpec((tm, s), lambda i, j: (j, 0)),
                  _const_spec(wf.shape), _const_spec(wl.shape), _const_spec(wo.shape),
                  _const_spec(bo.shape)],
        out_specs=[blk(d),
                   pl.BlockSpec((d, tm), lambda i, j: (0, i * nt + j))],
        out_shape=[jax.ShapeDtypeStruct((b, s, d), F32),
                   jax.ShapeDtypeStruct((d, b * s), BF16)],
        compiler_params=_cparams(("parallel", "arbitrary")),
        name="mix",
    )(x, pos, mod3, mod3, mod3, g2n, hf, hb, gy, sga, sgb, ab, ct, st, wf, wl, wo, bo)


def _batcher_pairs(n):
    pairs = []

    def merge(lo, m, r):
        step = r * 2
        if step < m:
            merge(lo, m, step)
            merge(lo + r, m, step)
            pairs.extend((i, i + r) for i in range(lo + r, lo + m - r, step))
        else:
            pairs.append((lo, lo + r))

    def sort(lo, m):
        if m > 1:
            sort(lo, m // 2)
            sort(lo + m // 2, m // 2)
            merge(lo, m, 1)

    sort(0, n)
    return pairs


_SORT16 = _batcher_pairs(PEER_TOPK)


def _exchange(a, i, j):
    hi, lo = jnp.maximum(a[i], a[j]), jnp.minimum(a[i], a[j])
    a[i], a[j] = hi, lo


def _sorted_top16(tiles):
    a = list(tiles)
    for i, j in _SORT16:
        _exchange(a, i, j)
    for shift in (4, 2, 1):
        b = [pltpu.roll(x, shift, axis=0) for x in a]
        a = [jnp.maximum(a[v], b[PEER_TOPK - 1 - v]) for v in range(PEER_TOPK)]
        for stride in (8, 4, 2, 1):
            for i in range(PEER_TOPK):
                if not i & stride:
                    _exchange(a, i, i + stride)
    return a


def _top16(s, kiota):
    rank = jnp.full(s.shape, float(PEER_TOPK), F32)
    for r in range(PEER_TOPK):
        m = jnp.max(s, axis=0, keepdims=True)
        first = jnp.min(jnp.where(s == m, kiota, float(N_KEYS)), axis=0, keepdims=True)
        hit = kiota == first
        s = jnp.where(hit, -jnp.inf, s)
        rank = jnp.where(hit, float(r), rank)
    return rank


def _route_kernel(h2t_ref, wqt_ref, keys_ref, rank2_ref, e2_ref, cnt_ref, coef_ref,
                  q_scr, s_scr, sv_scr, c_scr, z_scr, flag_scr, *, tr):
    nh = PEER_HEADS
    ntile = N_KEYS // SUBLANES
    nk16 = N_KEYS // BF16_ROWS
    q_scr[...] = jnp.dot(wqt_ref[...], h2t_ref[...], preferred_element_type=F32).astype(BF16)

    def scores(hp, _):
        off = pl.multiple_of(hp * PEER_HALF_DIM, PEER_HALF_DIM)
        s_scr[hp] = jnp.dot(keys_ref[hp], q_scr[pl.ds(off, PEER_HALF_DIM), :],
                            preferred_element_type=F32)
        return 0

    lax.fori_loop(0, 2 * nh, scores, 0)

    kiota = lax.broadcasted_iota(jnp.int32, (N_KEYS, LANES), 0).astype(F32)

    def lane_block(lb, _):
        l0 = pl.multiple_of(lb * LANES, LANES)
        lanes = pl.ds(l0, LANES)

        def sort_head(h, _):
            hrow = pl.ds(h, 1)
            for p in range(2):
                tiles = [s_scr[2 * h + p, v * SUBLANES:(v + 1) * SUBLANES, lanes]
                         for v in range(ntile)]
                sv = _sorted_top16(tiles)
                for r in range(PEER_TOPK):
                    sv_scr[p, r, hrow, :] = sv[r][0:1, :]
                nsel = jnp.zeros((SUBLANES, LANES), F32)
                for v in range(ntile):
                    nsel = nsel + jnp.where(tiles[v] >= sv[PEER_TOPK - 1], 1.0, 0.0)
                nsel = jnp.sum(nsel, axis=0, keepdims=True)
                flag_scr[p, hrow, :] = jnp.where(nsel != float(PEER_TOPK), 1.0, 0.0)
                if p == 1:
                    ranks = []
                    for v in range(ntile):
                        rank = jnp.zeros((SUBLANES, LANES), F32)
                        for r in range(PEER_TOPK):
                            rank = jnp.where(sv[r] > tiles[v], float(r + 1), rank)
                        ranks.append(rank)
                    for k in range(nk16):
                        pair = jnp.concatenate([ranks[2 * k], ranks[2 * k + 1]], axis=0)
                        rank2_ref[h, k, :, lanes] = pair.astype(BF16)
            return 0

        lax.fori_loop(0, nh, sort_head, 0)

        sv1 = [sv_scr[0, r] for r in range(PEER_TOPK)]
        sv2 = [sv_scr[1, r] for r in range(PEER_TOPK)]
        val = {c: sv1[c[0]] + sv2[c[1]] for c in _CELLS}
        before = {c: 0.0 for c in _CELLS}
        for ia, ca in enumerate(_CELLS):
            for cb in _CELLS[ia + 1:]:
                if ca[0] <= cb[0] and ca[1] <= cb[1]:
                    before[cb] = before[cb] + 1.0
                elif cb[0] <= ca[0] and cb[1] <= ca[1]:
                    before[ca] = before[ca] + 1.0
                else:
                    ge = jnp.where(val[ca] >= val[cb], 1.0, 0.0)
                    before[cb] = before[cb] + ge
                    before[ca] = before[ca] + (1.0 - ge)
        ex1 = [jnp.exp(sv1[r] - sv1[0]) for r in range(PEER_TOPK)]
        ex2 = [jnp.exp(sv2[r] - sv2[0]) for r in range(PEER_TOPK)]
        zsum = jnp.zeros((nh, LANES), F32)
        cnt = [jnp.zeros((nh, LANES), F32) for _ in range(PEER_TOPK)]
        for c in _CELLS:
            sel = before[c] < float(PEER_TOPK)
            cnt[c[0]] = cnt[c[0]] + jnp.where(sel, 1.0, 0.0)
            zsum = zsum + jnp.where(sel, ex1[c[0]] * ex2[c[1]], 0.0)
        for r in range(PEER_TOPK):
            c_scr[r] = cnt[r]
        z_scr[0] = 0.5 / zsum
        z_scr[1] = sv1[0]
        z_scr[2] = sv2[0]
        tie = jnp.maximum(flag_scr[0], flag_scr[1])
        for r in range(PEER_TOPK - 1):
            tie = jnp.where(sv1[r] == sv1[r + 1], 1.0, tie)
            tie = jnp.where(sv2[r] == sv2[r + 1], 1.0, tie)
        flag_scr[2] = tie

        def emit_head(h, _):
            hrow = pl.ds(h, 1)
            tile = lambda ref, *idx: jnp.broadcast_to(ref[(*idx, hrow, slice(None))],
                                                      (SUBLANES, LANES))
            sv1b = [tile(sv_scr, 0, r) for r in range(PEER_TOPK)]
            cb = [tile(c_scr, r) for r in range(PEER_TOPK)]
            gate, max1, max2 = tile(z_scr, 0), tile(z_scr, 1), tile(z_scr, 2)
            e2 = []
            for v in range(ntile):
                rows = slice(v * SUBLANES, (v + 1) * SUBLANES)
                s1 = s_scr[2 * h, rows, lanes]
                cnt1 = jnp.zeros((SUBLANES, LANES), F32)
                for r in range(PEER_TOPK):
                    cnt1 = jnp.where(s1 == sv1b[r], cb[r], cnt1)
                cnt_ref[h, rows, lanes] = cnt1
                coef_ref[h, rows, lanes] = jnp.exp(s1 - max1) * gate
                e2.append(jnp.exp(s_scr[2 * h + 1, rows, lanes] - max2))
            for k in range(nk16):
                pair = jnp.concatenate([e2[2 * k], e2[2 * k + 1]], axis=0)
                e2_ref[h, k, :, lanes] = pair.astype(BF16)
            return 0

        lax.fori_loop(0, nh, emit_head, 0)

        @pl.when(jnp.max(tie) > 0.0)
        def _():
            def fix_head(h, _):
                hrow = pl.ds(h, 1)

                @pl.when(jnp.max(flag_scr[2, hrow, :]) > 0.0)
                def _():
                    rank1 = _top16(s_scr[2 * h, :, lanes], kiota)
                    cnt1 = jnp.zeros((N_KEYS, LANES), F32)
                    for r in range(PEER_TOPK):
                        cnt1 = jnp.where(rank1 == float(r), c_scr[r, hrow, :], cnt1)
                    cnt_ref[h, :, lanes] = cnt1
                    rank2 = _top16(s_scr[2 * h + 1, :, lanes], kiota)
                    for k in range(nk16):
                        rows = slice(k * BF16_ROWS, (k + 1) * BF16_ROWS)
                        rank2_ref[h, k, :, lanes] = rank2[rows].astype(BF16)

                return 0

            lax.fori_loop(0, nh, fix_head, 0)

        return 0

    lax.fori_loop(0, tr // LANES, lane_block, 0)


def _route(h2t, wqt, keys):
    d, t = h2t.shape
    nh = PEER_HEADS
    tr = 256
    nk16 = N_KEYS // BF16_ROWS
    kern = functools.partial(_route_kernel, tr=tr)
    bblk = pl.BlockSpec((nh, nk16, BF16_ROWS, tr), lambda i: (0, 0, 0, i))
    fblk = pl.BlockSpec((nh, N_KEYS, tr), lambda i: (0, 0, i))
    return pl.pallas_call(
        kern,
        grid=(t // tr,),
        in_specs=[pl.BlockSpec((d, tr), lambda i: (0, i)),
                  _const_spec(wqt.shape),
                  _const_spec(keys.shape)],
        out_specs=[bblk, bblk, fblk, fblk],
        out_shape=[jax.ShapeDtypeStruct((nh, nk16, BF16_ROWS, t), BF16),
                   jax.ShapeDtypeStruct((nh, nk16, BF16_ROWS, t), BF16),
                   jax.ShapeDtypeStruct((nh, N_KEYS, t), F32),
                   jax.ShapeDtypeStruct((nh, N_KEYS, t), F32)],
        scratch_shapes=[pltpu.VMEM((2 * nh * PEER_HALF_DIM, tr), BF16),
                        pltpu.VMEM((2 * nh, N_KEYS, tr), F32),
                        pltpu.VMEM((2, PEER_TOPK, nh, LANES), F32),
                        pltpu.VMEM((PEER_TOPK, nh, LANES), F32),
                        pltpu.VMEM((3, nh, LANES), F32),
                        pltpu.VMEM((3, nh, LANES), F32)],
        compiler_params=_cparams(("parallel",)),
        name="route",
    )(h2t, wqt, keys)


def _peer_kernel(h2t_ref, u_ref, vt_ref, rank2_ref, e2_ref, cnt_ref, coef_ref, x1_ref,
                 g2m_ref, gf_ref, o_ref, acc_ref, wt_ref, *, eb, ne, tm):
    e = pl.program_id(1)
    nk16 = N_KEYS // BF16_ROWS

    @pl.when(e == 0)
    def _():
        acc_ref[...] = jnp.zeros_like(acc_ref)

    st = jnp.dot(u_ref[...], h2t_ref[...], preferred_element_type=F32)
    for k in range(eb // N_KEYS):
        i1 = e * (eb // N_KEYS) + k
        cnt = [jnp.broadcast_to(cnt_ref[h, pl.ds(i1, 1), :], (BF16_ROWS, tm)).astype(BF16)
               for h in range(PEER_HEADS)]
        cf = [jnp.broadcast_to(coef_ref[h, pl.ds(i1, 1), :], (BF16_ROWS, tm)).astype(BF16)
              for h in range(PEER_HEADS)]
        for c in range(nk16):
            m = None
            for h in range(PEER_HEADS):
                term = jnp.where(rank2_ref[h, c] < cnt[h], e2_ref[h, c] * cf[h],
                                 jnp.zeros((), BF16))
                m = term if m is None else m + term
            r0 = k * N_KEYS + c * BF16_ROWS
            act2 = _gelu_x2(st[r0:r0 + BF16_ROWS, :])
            wt_ref[r0:r0 + BF16_ROWS, :] = m * act2.astype(BF16)
    acc_ref[...] += jnp.dot(vt_ref[...], wt_ref[...], preferred_element_type=F32)

    @pl.when(e == ne - 1)
    def _():
        x2 = x1_ref[...] + g2m_ref[...] * acc_ref[...].T
        ms = jnp.mean(x2 * x2, axis=-1, keepdims=True)
        o_ref[...] = (x2 * lax.rsqrt(ms + NORM_EPS)) * gf_ref[...]


def _peer(h2t, u, vt, rank2, e2, cnt, coef, x1, mod3, gf):
    b, s, d = x1.shape
    t = b * s
    ne_total = u.shape[0]
    tm = min(512, s)
    eb = 1024
    ne = ne_total // eb
    per_b = s // tm
    nk16 = N_KEYS // BF16_ROWS
    kern = functools.partial(_peer_kernel, eb=eb, ne=ne, tm=tm)
    bblk = pl.BlockSpec((PEER_HEADS, nk16, BF16_ROWS, tm), lambda i, e: (0, 0, 0, i))
    fblk = pl.BlockSpec((PEER_HEADS, N_KEYS, tm), lambda i, e: (0, 0, i))
    xblk = pl.BlockSpec((None, tm, d), lambda i, e: (i // per_b, i % per_b, 0))
    return pl.pallas_call(
        kern,
        grid=(t // tm, ne),
        in_specs=[pl.BlockSpec((d, tm), lambda i, e: (0, i)),
                  pl.BlockSpec((eb, d), lambda i, e: (e, 0)),
                  pl.BlockSpec((d, eb), lambda i, e: (0, e)),
                  bblk, bblk, fblk, fblk,
                  xblk,
                  pl.BlockSpec((None, 1, d), lambda i, e: (i // per_b, 0, 5)),
                  pl.BlockSpec((1, d), lambda i, e: (0, 0))],
        out_specs=xblk,
        out_shape=jax.ShapeDtypeStruct((b, s, d), F32),
        scratch_shapes=[pltpu.VMEM((d, tm), F32), pltpu.VMEM((eb, tm), BF16)],
        compiler_params=_cparams(("parallel", "arbitrary")),
        name="peer",
    )(h2t, u, vt, rank2, e2, cnt, coef, x1, mod3, gf)


def _sincos_2d(num_tokens, dim):
    rows = num_tokens // GRID_W
    row = jnp.repeat(jnp.arange(rows, dtype=F32), GRID_W)
    col = jnp.tile(jnp.arange(GRID_W, dtype=F32), rows)
    nf = dim // 4
    omega = 1.0 / (POS_BASE ** (jnp.arange(nf, dtype=F32) / nf))
    er = row[:, None] * omega[None]
    ec = col[:, None] * omega[None]
    return jnp.concatenate([jnp.sin(er), jnp.cos(er), jnp.sin(ec), jnp.cos(ec)], axis=-1)


def _dft_tables(n):
    k = jnp.arange(n, dtype=jnp.int32)
    ang = ((k[:, None] * k[None, :]) % n).astype(F32) * (2.0 * math.pi / n)
    return jnp.cos(ang), jnp.sin(ang)


def kernel(x, c, ctx, c_ctx, w_mod, b_mod, norm1_g, norm2_g, w_in, w_fourier, conv_w, conv_b,
           lru_wa, lru_ba, lru_wx, lru_bx, lru_lambda, w_lru_out, w_out, b_out,
           peer_wq, peer_keys, peer_u, peer_v, final_g):
    b, s, d = x.shape
    ctx_len = ctx.shape[1]
    f = FOURIER_GROUPS * FOURIER_GROUP_DIM
    r = LRU_HEADS * LRU_HEAD_DIM
    assert w_mod.shape[0] == 1, "single-layer block"
    assert w_in.shape[2] == f + 2 * r + 2 * d

    pos = _sincos_2d(s, d)
    ct, st = _dft_tables(s)
    cc, sc = _dft_tables(FOURIER_GROUP_DIM)
    scale = 1.0 / math.sqrt(s * FOURIER_GROUP_DIM)
    eye = jnp.eye(FOURIER_GROUPS, dtype=F32)
    wc = jnp.concatenate([jnp.kron(eye, cc), jnp.kron(eye, sc)], axis=1) * scale

    rp = -(-(b + 1) // SUBLANES) * SUBLANES
    cond = jnp.concatenate([c, c_ctx[None], jnp.zeros((rp - b - 1, d), F32)], axis=0)
    mod = _adaln(cond, w_mod[0], b_mod)
    mod3 = mod.reshape(rp, 1, 6 * d)

    w_in_b = w_in[0].astype(BF16)
    g1 = norm1_g
    g2 = norm2_g
    wa = lru_wa[0].astype(BF16)
    wx = lru_wx[0].astype(BF16)
    ba, bx, lam = lru_ba[0], lru_bx[0], lru_lambda[0]

    def scans(zx, h0f, h0b):
        hf = _scan(zx, conv_w[0], conv_b, wa[0], ba[0:1], wx[0], bx[0:1], lam[0:1], h0f, False)
        hb = _scan(zx, conv_w[0], conv_b, wa[1], ba[1:2], wx[1], bx[1:2], lam[1:2], h0b, True)
        return hf, hb

    zx_c = _ctx_proj(ctx.reshape(b * ctx_len, d), mod3, b, g1, w_in_b[:, f:f + r])
    zeros = jnp.zeros((b, 1, r), F32)
    hf_c, hb_c = scans(zx_c.reshape(b, ctx_len, r), zeros, zeros)
    htf = hf_c[:, ctx_len - 1:ctx_len, :]
    htb = hb_c[:, 0:1, :]

    ab, zx, gy, sga, sgb = _proj_in(x, pos, mod3, g1, w_in_b, wc.astype(BF16), f, r)
    hf, hb = scans(zx, htf, htb)
    x1, h2t = _mix(x, pos, mod3, g2, hf, hb, gy, sga, sgb, ab,
                   ct.astype(BF16), st.astype(BF16),
                   w_fourier[0].astype(BF16), w_lru_out[0].astype(BF16),
                   w_out[0].astype(BF16), b_out, f)

    wqt = peer_wq[0].T.astype(BF16)
    keys = peer_keys[0].reshape(2 * PEER_HEADS, N_KEYS, PEER_HALF_DIM).astype(BF16)
    rank2, e2, cnt, coef = _route(h2t, wqt, keys)
    u = peer_u[0].astype(BF16)
    vt = peer_v[0].T.astype(BF16)
    return _peer(h2t, u, vt, rank2, e2, cnt, coef, x1, mod3, final_g[None])
```

```python
import functools
import math

import jax
import jax.numpy as jnp
from jax import lax
from jax.experimental import pallas as pl
from jax.experimental.pallas import tpu as pltpu

F32 = jnp.float32
BF16 = jnp.bfloat16

NORM_EPS = 1e-6
POS_BASE = 10000.0
GRID_W = 64
FOURIER_GROUPS = 4
FOURIER_GROUP_DIM = 128
LRU_HEADS = 8
LRU_HEAD_DIM = 128
CONV_WIDTH = 4
LRU_C = 8.0
PEER_HEADS = 8
PEER_HALF_DIM = 128
N_KEYS = 128
PEER_TOPK = 16

LANES = 128
SUBLANES = 8
BF16_ROWS = 16
VMEM_LIMIT = 56 * 1024 * 1024

_CELLS = [(r1, r2) for r1 in range(PEER_TOPK) for r2 in range(PEER_TOPK)
          if (r1 + 1) * (r2 + 1) <= PEER_TOPK]


def _cparams(sem):
    return pltpu.CompilerParams(dimension_semantics=sem, vmem_limit_bytes=VMEM_LIMIT)


def _const_spec(shape):
    nd = len(shape)
    return pl.BlockSpec(shape, lambda *_: (0,) * nd, pipeline_mode=pl.Buffered(1))


def _gelu(x):
    c = math.sqrt(2.0 / math.pi)
    return x * (0.5 * (1.0 + jnp.tanh(c * (x + 0.044715 * (x * x * x)))))


def _gelu_x2(x):
    c = math.sqrt(2.0 / math.pi)
    return x * (1.0 + jnp.tanh(x * (c + (c * 0.044715) * (x * x))))


def _norm_mod(x, g, sh, sc):
    ms = jnp.mean(x * x, axis=-1, keepdims=True)
    y = x * lax.rsqrt(ms + NORM_EPS)
    return (y * g) * (1.0 + sc) + sh


def _adaln_kernel(c_ref, w_ref, b_ref, o_ref):
    c = c_ref[...]
    s = c * jax.nn.sigmoid(c)
    o_ref[...] = jnp.dot(s, w_ref[...], preferred_element_type=F32,
                         precision=lax.Precision.HIGHEST) + b_ref[...]


def _adaln(cc, w_mod, b_mod):
    rp, d = cc.shape
    n = w_mod.shape[1]
    tn = 1536
    return pl.pallas_call(
        _adaln_kernel,
        grid=(n // tn,),
        in_specs=[pl.BlockSpec((rp, d), lambda j: (0, 0)),
                  pl.BlockSpec((d, tn), lambda j: (0, j)),
                  pl.BlockSpec((1, tn), lambda j: (0, j))],
        out_specs=pl.BlockSpec((rp, tn), lambda j: (0, j)),
        out_shape=jax.ShapeDtypeStruct((rp, n), F32),
        compiler_params=_cparams(("parallel",)),
        name="adaln",
    )(cc, w_mod, b_mod)


def _ctx_proj_kernel(x_ref, sh_ref, sc_ref, g_ref, w_ref, o_ref):
    h = _norm_mod(x_ref[...], g_ref[...], sh_ref[...], sc_ref[...])
    o_ref[...] = jnp.dot(h.astype(BF16), w_ref[...], preferred_element_type=F32)


def _ctx_proj(ctx2, mod3, row, g1, w_zx):
    n, d = ctx2.shape
    r = w_zx.shape[1]
    tt = min(512, n)
    return pl.pallas_call(
        _ctx_proj_kernel,
        grid=(n // tt,),
        in_specs=[pl.BlockSpec((tt, d), lambda i: (i, 0)),
                  pl.BlockSpec((None, 1, d), lambda i: (row, 0, 0)),
                  pl.BlockSpec((None, 1, d), lambda i: (row, 0, 1)),
                  pl.BlockSpec((1, d), lambda i: (0, 0)),
                  _const_spec((d, r))],
        out_specs=pl.BlockSpec((tt, r), lambda i: (i, 0)),
        out_shape=jax.ShapeDtypeStruct((n, r), F32),
        compiler_params=_cparams(("parallel",)),
        name="ctx_proj",
    )(ctx2, mod3, mod3, g1, w_zx)


def _scan_kernel(zc_ref, zp_ref, zn_ref, cw_ref, cb_ref, wa_ref, ba_ref, wx_ref, bx_ref,
                 lam_ref, h0_ref, o_ref, a_scr, b_scr, h_scr, *, reverse, nt, ts):
    j = pl.program_id(1)
    jj = (nt - 1 - j) if reverse else j

    @pl.when(j == 0)
    def _():
        h_scr[...] = h0_ref[...]

    cur = zc_ref[...]
    has_prev = (jj > 0).astype(F32)
    has_next = (jj < nt - 1).astype(F32)
    prev_row = zp_ref[SUBLANES - 1:SUBLANES, :] * has_prev
    next0 = zn_ref[0:1, :] * has_next
    next1 = zn_ref[1:2, :] * has_next
    row = lax.broadcasted_iota(jnp.int32, cur.shape, 0)
    zm1 = jnp.where(row == 0, prev_row, pltpu.roll(cur, 1, axis=0))
    zp1 = jnp.where(row == ts - 1, next0, pltpu.roll(cur, ts - 1, axis=0))
    zp2 = pltpu.roll(cur, ts - 2, axis=0)
    zp2 = jnp.where(row == ts - 2, next0, jnp.where(row == ts - 1, next1, zp2))
    xc = cb_ref[...] + zm1 * cw_ref[0:1, :]
    xc = xc + cur * cw_ref[1:2, :]
    xc = xc + zp1 * cw_ref[2:3, :]
    xc = xc + zp2 * cw_ref[3:4, :]

    lam = lam_ref[...]
    softplus_neg = jnp.maximum(-lam, 0.0) + jnp.log1p(jnp.exp(-jnp.abs(lam)))
    for hd in range(LRU_HEADS):
        cols = slice(hd * LRU_HEAD_DIM, (hd + 1) * LRU_HEAD_DIM)
        xh = xc[:, cols]
        xb = xh.astype(BF16)
        r = jax.nn.sigmoid(jnp.dot(xb, wa_ref[hd], preferred_element_type=F32) + ba_ref[:, cols])
        i = jax.nn.sigmoid(jnp.dot(xb, wx_ref[hd], preferred_element_type=F32) + bx_ref[:, cols])
        log_a = (-LRU_C * r) * softplus_neg[:, cols]
        a = jnp.exp(log_a)
        a_scr[:, cols] = a
        b_scr[:, cols] = jnp.sqrt(jnp.tanh(-log_a) * (a * a + 1.0)) * (i * xh)

    def step(k, h):
        t = (ts - 1 - k) if reverse else k
        h = a_scr[pl.ds(t, 1), :] * h + b_scr[pl.ds(t, 1), :]
        o_ref[pl.ds(t, 1), :] = h
        return h

    h_scr[...] = lax.fori_loop(0, ts, step, h_scr[...], unroll=8)


def _scan(zx, conv_w, conv_b, wa, ba, wx, bx, lam, h0, reverse):
    b, s, r = zx.shape
    ts = min(256, s)
    nt = s // ts
    nb8 = s // SUBLANES
    per = ts // SUBLANES

    def jj(j):
        return (nt - 1 - j) if reverse else j

    kern = functools.partial(_scan_kernel, reverse=reverse, nt=nt, ts=ts)
    return pl.pallas_call(
        kern,
        grid=(b, nt),
        in_specs=[pl.BlockSpec((None, ts, r), lambda i, j: (i, jj(j), 0)),
                  pl.BlockSpec((None, SUBLANES, r),
                               lambda i, j: (i, jnp.maximum(jj(j) * per - 1, 0), 0)),
                  pl.BlockSpec((None, SUBLANES, r),
                               lambda i, j: (i, jnp.minimum((jj(j) + 1) * per, nb8 - 1), 0)),
                  _const_spec((CONV_WIDTH, r)),
                  _const_spec((1, r)),
                  _const_spec((LRU_HEADS, LRU_HEAD_DIM, LRU_HEAD_DIM)),
                  _const_spec((1, r)),
                  _const_spec((LRU_HEADS, LRU_HEAD_DIM, LRU_HEAD_DIM)),
                  _const_spec((1, r)),
                  _const_spec((1, r)),
                  pl.BlockSpec((None, 1, r), lambda i, j: (i, 0, 0))],
        out_specs=pl.BlockSpec((None, ts, r), lambda i, j: (i, jj(j), 0)),
        out_shape=jax.ShapeDtypeStruct((b, s, r), F32),
        scratch_shapes=[pltpu.VMEM((ts, r), F32), pltpu.VMEM((ts, r), F32),
                        pltpu.VMEM((1, r), F32)],
        compiler_params=_cparams(("arbitrary", "arbitrary")),
        name="scan_bwd" if reverse else "scan_fwd",
    )(zx, zx, zx, conv_w, conv_b, wa, ba, wx, bx, lam, h0)


def _proj_in_kernel(x_ref, pos_ref, sh_ref, sc_ref, g_ref, w_ref, wc_ref,
                    ab_ref, zx_ref, gy_ref, sga_ref, sgb_ref, *, f, r, d):
    xp = x_ref[...] + pos_ref[...]
    h = _norm_mod(xp, g_ref[...], sh_ref[...], sc_ref[...]).astype(BF16)
    zf = jnp.dot(h, w_ref[:, 0:f], preferred_element_type=F32)
    ab_ref[...] = jnp.dot(zf.astype(BF16), wc_ref[...], preferred_element_type=F32).astype(BF16)
    zx_ref[...] = jnp.dot(h, w_ref[:, f:f + r], preferred_element_type=F32)
    zy = jnp.dot(h, w_ref[:, f + r:f + 2 * r], preferred_element_type=F32)
    gy_ref[...] = _gelu(zy).astype(BF16)
    zga = jnp.dot(h, w_ref[:, f + 2 * r:f + 2 * r + d], preferred_element_type=F32)
    sga_ref[...] = jax.nn.sigmoid(zga).astype(BF16)
    zgb = jnp.dot(h, w_ref[:, f + 2 * r + d:f + 2 * r + 2 * d], preferred_element_type=F32)
    sgb_ref[...] = jax.nn.sigmoid(zgb).astype(BF16)


def _proj_in(x, pos, mod3, g1, w_in, wc, f, r):
    b, s, d = x.shape
    tt = min(512, s)
    nt = s // tt
    kern = functools.partial(_proj_in_kernel, f=f, r=r, d=d)
    blk = lambda w: pl.BlockSpec((None, tt, w), lambda j, i: (i, j, 0))
    return pl.pallas_call(
        kern,
        grid=(nt, b),
        in_specs=[blk(d),
                  pl.BlockSpec((tt, d), lambda j, i: (j, 0)),
                  pl.BlockSpec((None, 1, d), lambda j, i: (i, 0, 0)),
                  pl.BlockSpec((None, 1, d), lambda j, i: (i, 0, 1)),
                  pl.BlockSpec((1, d), lambda j, i: (0, 0)),
                  _const_spec(w_in.shape),
                  _const_spec(wc.shape)],
        out_specs=[blk(2 * f), blk(r), blk(r), blk(d), blk(d)],
        out_shape=[jax.ShapeDtypeStruct((b, s, 2 * f), BF16),
                   jax.ShapeDtypeStruct((b, s, r), F32),
                   jax.ShapeDtypeStruct((b, s, r), BF16),
                   jax.ShapeDtypeStruct((b, s, d), BF16),
                   jax.ShapeDtypeStruct((b, s, d), BF16)],
        compiler_params=_cparams(("parallel", "parallel")),
        name="proj_in",
    )(x, pos, mod3, mod3, g1, w_in, wc)


def _mix_kernel(x_ref, pos_ref, g1m_ref, sh2_ref, sc2_ref, g2n_ref, hf_ref, hb_ref, gy_ref,
                sga_ref, sgb_ref, ab_ref, ct_ref, st_ref, wf_ref, wl_ref, wo_ref, bo_ref,
                x1_ref, h2t_ref, *, f):
    yf = jnp.dot(ct_ref[...], ab_ref[:, 0:f], preferred_element_type=F32)
    yf = yf - jnp.dot(st_ref[...], ab_ref[:, f:2 * f], preferred_element_type=F32)
    ya = jnp.dot(yf.astype(BF16), wf_ref[...], preferred_element_type=F32)
    hs = (hf_ref[...] + hb_ref[...]) * gy_ref[...].astype(F32)
    yb = jnp.dot(hs.astype(BF16), wl_ref[...], preferred_element_type=F32)
    merged = sga_ref[...].astype(F32) * ya + sgb_ref[...].astype(F32) * yb
    mix = jnp.dot(merged.astype(BF16), wo_ref[...], preferred_element_type=F32) + bo_ref[...]
    x1 = (x_ref[...] + pos_ref[...]) + g1m_ref[...] * mix
    x1_ref[...] = x1
    h2 = _norm_mod(x1, g2n_ref[...], sh2_ref[...], sc2_ref[...])
    h2t_ref[...] = h2.T.astype(BF16)


def _mix(x, pos, mod3, g2n, hf, hb, gy, sga, sgb, ab, ct, st, wf, wl, wo, bo, f):
    b, s, d = x.shape
    r = hf.shape[-1]
    tm = min(256, s)
    nt = s // tm
    kern = functools.partial(_mix_kernel, f=f)
    blk = lambda w: pl.BlockSpec((None, tm, w), lambda i, j: (i, j, 0))
    modrow = lambda k: pl.BlockSpec((None, 1, d), lambda i, j: (i, 0, k))
    return pl.pallas_call(
        kern,
        grid=(b, nt),
        in_specs=[blk(d),
                  pl.BlockSpec((tm, d), lambda i, j: (j, 0)),
                  modrow(2), modrow(3), modrow(4),
                  pl.BlockSpec((1, d), lambda i, j: (0, 0)),
                  blk(r), blk(r), blk(r), blk(d), blk(d),
                  pl.BlockSpec((None, s, 2 * f), lambda i, j: (i, 0, 0),
                               pipeline_mode=pl.Buffered(1)),
                  pl.BlockSpec((tm, s), lambda i, j: (j, 0)),
                  pl.BlockSpec((tm, s), lambda i, j: (j, 0)),
                  _const_spec(wf.shape), _const_spec(wl.shape), _const_spec(wo.shape),
                  _const_spec(bo.shape)],
        out_specs=[blk(d),
                   pl.BlockSpec((d, tm), lambda i, j: (0, i * nt + j))],
        out_shape=[jax.ShapeDtypeStruct((b, s, d), F32),
                   jax.ShapeDtypeStruct((d, b * s), BF16)],
        compiler_params=_cparams(("parallel", "arbitrary")),
        name="mix",
    )(x, pos, mod3, mod3, mod3, g2n, hf, hb, gy, sga, sgb, ab, ct, st, wf, wl, wo, bo)


def _batcher_pairs(n):
    pairs = []

    def merge(lo, m, r):
        step = r * 2
        if step < m:
            merge(lo, m, step)
            merge(lo + r, m, step)
            pairs.extend((i, i + r) for i in range(lo + r, lo + m - r, step))
        else:
            pairs.append((lo, lo + r))

    def sort(lo, m):
        if m > 1:
            sort(lo, m // 2)
            sort(lo + m // 2, m // 2)
            merge(lo, m, 1)

    sort(0, n)
    return pairs


_SORT16 = _batcher_pairs(PEER_TOPK)


def _exchange(a, i, j):
    hi, lo = jnp.maximum(a[i], a[j]), jnp.minimum(a[i], a[j])
    a[i], a[j] = hi, lo


def _sorted_top16(tiles):
    a = list(tiles)
    for i, j in _SORT16:
        _exchange(a, i, j)
    for shift in (4, 2, 1):
        b = [pltpu.roll(x, shift, axis=0) for x in a]
        a = [jnp.maximum(a[v], b[PEER_TOPK - 1 - v]) for v in range(PEER_TOPK)]
        for stride in (8, 4, 2, 1):
            for i in range(PEER_TOPK):
                if not i & stride:
                    _exchange(a, i, i + stride)
    return a


def _top16(s, kiota):
    rank = jnp.full(s.shape, float(PEER_TOPK), F32)
    for r in range(PEER_TOPK):
        m = jnp.max(s, axis=0, keepdims=True)
        first = jnp.min(jnp.where(s == m, kiota, float(N_KEYS)), axis=0, keepdims=True)
        hit = kiota == first
        s = jnp.where(hit, -jnp.inf, s)
        rank = jnp.where(hit, float(r), rank)
    return rank


def _keyproj_kernel(keys_ref, wq_ref, o_ref):
    o_ref[...] = lax.dot_general(keys_ref[...], wq_ref[...], (((1,), (1,)), ((), ())),
                                 preferred_element_type=F32,
                                 precision=lax.Precision.HIGHEST).astype(BF16)


def _keyproj(keys, wq):
    nhp, nk, hd = keys.shape
    d = wq.shape[0]
    return pl.pallas_call(
        _keyproj_kernel,
        grid=(nhp,),
        in_specs=[pl.BlockSpec((None, nk, hd), lambda i: (i, 0, 0)),
                  pl.BlockSpec((d, hd), lambda i: (0, i))],
        out_specs=pl.BlockSpec((nk, d), lambda i: (i, 0)),
        out_shape=jax.ShapeDtypeStruct((nhp * nk, d), BF16),
        compiler_params=_cparams(("parallel",)),
        name="keyproj",
    )(keys, wq)


def _route_kernel(h2t_ref, kw_ref, rank2_ref, e2_ref, cnt_ref, coef_ref,
                  s_scr, sv_scr, c_scr, z_scr, flag_scr, *, tr):
    nh = PEER_HEADS
    ntile = N_KEYS // SUBLANES
    nk16 = N_KEYS // BF16_ROWS
    s_all = jnp.dot(kw_ref[...], h2t_ref[...], preferred_element_type=F32)
    for hp in range(2 * nh):
        s_scr[hp] = s_all[hp * N_KEYS:(hp + 1) * N_KEYS, :]

    kiota = lax.broadcasted_iota(jnp.int32, (N_KEYS, LANES), 0).astype(F32)

    def lane_block(lb, _):
        l0 = pl.multiple_of(lb * LANES, LANES)
        lanes = pl.ds(l0, LANES)

        def sort_head(h, _):
            hrow = pl.ds(h, 1)
            for p in range(2):
                tiles = [s_scr[2 * h + p, v * SUBLANES:(v + 1) * SUBLANES, lanes]
                         for v in range(ntile)]
                sv = _sorted_top16(tiles)
                for r in range(PEER_TOPK):
                    sv_scr[p, r, hrow, :] = sv[r][0:1, :]
                nsel = jnp.zeros((SUBLANES, LANES), F32)
                for v in range(ntile):
                    nsel = nsel + jnp.where(tiles[v] >= sv[PEER_TOPK - 1], 1.0, 0.0)
                nsel = jnp.sum(nsel, axis=0, keepdims=True)
                flag_scr[p, hrow, :] = jnp.where(nsel != float(PEER_TOPK), 1.0, 0.0)
                if p == 1:
                    ranks = []
                    for v in range(ntile):
                        rank = jnp.zeros((SUBLANES, LANES), F32)
                        for r in range(PEER_TOPK):
                            rank = jnp.where(sv[r] > tiles[v], float(r + 1), rank)
                        ranks.append(rank)
                    for k in range(nk16):
                        pair = jnp.concatenate([ranks[2 * k], ranks[2 * k + 1]], axis=0)
                        rank2_ref[h, k, :, lanes] = pair.astype(BF16)
            return 0

        lax.fori_loop(0, nh, sort_head, 0)

        sv1 = [sv_scr[0, r] for r in range(PEER_TOPK)]
        sv2 = [sv_scr[1, r] for r in range(PEER_TOPK)]
        val = {c: sv1[c[0]] + sv2[c[1]] for c in _CELLS}
        before = {c: 0.0 for c in _CELLS}
        for ia, ca in enumerate(_CELLS):
            for cb in _CELLS[ia + 1:]:
                if ca[0] <= cb[0] and ca[1] <= cb[1]:
                    before[cb] = before[cb] + 1.0
                elif cb[0] <= ca[0] and cb[1] <= ca[1]:
                    before[ca] = before[ca] + 1.0
                else:
                    ge = jnp.where(val[ca] >= val[cb], 1.0, 0.0)
                    before[cb] = before[cb] + ge
                    before[ca] = before[ca] + (1.0 - ge)
        ex1 = [jnp.exp(sv1[r] - sv1[0]) for r in range(PEER_TOPK)]
        ex2 = [jnp.exp(sv2[r] - sv2[0]) for r in range(PEER_TOPK)]
        zsum = jnp.zeros((nh, LANES), F32)
        cnt = [jnp.zeros((nh, LANES), F32) for _ in range(PEER_TOPK)]
        for c in _CELLS:
            sel = before[c] < float(PEER_TOPK)
            cnt[c[0]] = cnt[c[0]] + jnp.where(sel, 1.0, 0.0)
            zsum = zsum + jnp.where(sel, ex1[c[0]] * ex2[c[1]], 0.0)
        for r in range(PEER_TOPK):
            c_scr[r] = cnt[r]
        z_scr[0] = 0.5 / zsum
        z_scr[1] = sv1[0]
        z_scr[2] = sv2[0]
        tie = jnp.maximum(flag_scr[0], flag_scr[1])
        for r in range(PEER_TOPK - 1):
            tie = jnp.where(sv1[r] == sv1[r + 1], 1.0, tie)
            tie = jnp.where(sv2[r] == sv2[r + 1], 1.0, tie)
        flag_scr[2] = tie

        def emit_head(h, _):
            hrow = pl.ds(h, 1)
            tile = lambda ref, *idx: jnp.broadcast_to(ref[(*idx, hrow, slice(None))],
                                                      (SUBLANES, LANES))
            sv1b = [tile(sv_scr, 0, r) for r in range(PEER_TOPK)]
            cb = [tile(c_scr, r) for r in range(PEER_TOPK)]
            gate, max1, max2 = tile(z_scr, 0), tile(z_scr, 1), tile(z_scr, 2)
            e2 = []
            for v in range(ntile):
                rows = slice(v * SUBLANES, (v + 1) * SUBLANES)
                s1 = s_scr[2 * h, rows, lanes]
                cnt1 = jnp.zeros((SUBLANES, LANES), F32)
                for r in range(PEER_TOPK):
                    cnt1 = jnp.where(s1 == sv1b[r], cb[r], cnt1)
                cnt_ref[h, rows, lanes] = cnt1
                coef_ref[h, rows, lanes] = jnp.exp(s1 - max1) * gate
                e2.append(jnp.exp(s_scr[2 * h + 1, rows, lanes] - max2))
            for k in range(nk16):
                pair = jnp.concatenate([e2[2 * k], e2[2 * k + 1]], axis=0)
                e2_ref[h, k, :, lanes] = pair.astype(BF16)
            return 0

        lax.fori_loop(0, nh, emit_head, 0)

        @pl.when(jnp.max(tie) > 0.0)
        def _():
            def fix_head(h, _):
                hrow = pl.ds(h, 1)

                @pl.when(jnp.max(flag_scr[2, hrow, :]) > 0.0)
                def _():
                    rank1 = _top16(s_scr[2 * h, :, lanes], kiota)
                    cnt1 = jnp.zeros((N_KEYS, LANES), F32)
                    for r in range(PEER_TOPK):
                        cnt1 = jnp.where(rank1 == float(r), c_scr[r, hrow, :], cnt1)
                    cnt_ref[h, :, lanes] = cnt1
                    rank2 = _top16(s_scr[2 * h + 1, :, lanes], kiota)
                    for k in range(nk16):
                        rows = slice(k * BF16_ROWS, (k + 1) * BF16_ROWS)
                        rank2_ref[h, k, :, lanes] = rank2[rows].astype(BF16)

                return 0

            lax.fori_loop(0, nh, fix_head, 0)

        return 0

    lax.fori_loop(0, tr // LANES, lane_block, 0)


def _route(h2t, kw):
    d, t = h2t.shape
    nh = PEER_HEADS
    tr = 256
    nk16 = N_KEYS // BF16_ROWS
    kern = functools.partial(_route_kernel, tr=tr)
    bblk = pl.BlockSpec((nh, nk16, BF16_ROWS, tr), lambda i: (0, 0, 0, i))
    fblk = pl.BlockSpec((nh, N_KEYS, tr), lambda i: (0, 0, i))
    return pl.pallas_call(
        kern,
        grid=(t // tr,),
        in_specs=[pl.BlockSpec((d, tr), lambda i: (0, i)),
                  _const_spec(kw.shape)],
        out_specs=[bblk, bblk, fblk, fblk],
        out_shape=[jax.ShapeDtypeStruct((nh, nk16, BF16_ROWS, t), BF16),
                   jax.ShapeDtypeStruct((nh, nk16, BF16_ROWS, t), BF16),
                   jax.ShapeDtypeStruct((nh, N_KEYS, t), F32),
                   jax.ShapeDtypeStruct((nh, N_KEYS, t), F32)],
        scratch_shapes=[pltpu.VMEM((2 * nh, N_KEYS, tr), F32),
                        pltpu.VMEM((2, PEER_TOPK, nh, LANES), F32),
                        pltpu.VMEM((PEER_TOPK, nh, LANES), F32),
                        pltpu.VMEM((3, nh, LANES), F32),
                        pltpu.VMEM((3, nh, LANES), F32)],
        compiler_params=_cparams(("parallel",)),
        name="route",
    )(h2t, kw)


def _peer_kernel(h2t_ref, u_ref, vt_ref, rank2_ref, e2_ref, cnt_ref, coef_ref, x1_ref,
                 g2m_ref, gf_ref, o_ref, acc_ref, wt_ref, *, eb, ne, tm):
    e = pl.program_id(1)
    nk16 = N_KEYS // BF16_ROWS

    @pl.when(e == 0)
    def _():
        acc_ref[...] = jnp.zeros_like(acc_ref)

    st = jnp.dot(u_ref[...], h2t_ref[...], preferred_element_type=F32)
    for k in range(eb // N_KEYS):
        i1 = e * (eb // N_KEYS) + k
        cnt = [jnp.broadcast_to(cnt_ref[h, pl.ds(i1, 1), :], (BF16_ROWS, tm)).astype(BF16)
               for h in range(PEER_HEADS)]
        cf = [jnp.broadcast_to(coef_ref[h, pl.ds(i1, 1), :], (BF16_ROWS, tm)).astype(BF16)
              for h in range(PEER_HEADS)]
        for c in range(nk16):
            m = None
            for h in range(PEER_HEADS):
                term = jnp.where(rank2_ref[h, c] < cnt[h], e2_ref[h, c] * cf[h],
                                 jnp.zeros((), BF16))
                m = term if m is None else m + term
            r0 = k * N_KEYS + c * BF16_ROWS
            act2 = _gelu_x2(st[r0:r0 + BF16_ROWS, :].astype(BF16))
            wt_ref[r0:r0 + BF16_ROWS, :] = m * act2
    acc_ref[...] += jnp.dot(vt_ref[...], wt_ref[...], preferred_element_type=F32)

    @pl.when(e == ne - 1)
    def _():
        x2 = x1_ref[...] + g2m_ref[...] * acc_ref[...].T
        ms = jnp.mean(x2 * x2, axis=-1, keepdims=True)
        o_ref[...] = (x2 * lax.rsqrt(ms + NORM_EPS)) * gf_ref[...]


def _peer(h2t, u, vt, rank2, e2, cnt, coef, x1, mod3, gf):
    b, s, d = x1.shape
    t = b * s
    ne_total = u.shape[0]
    tm = min(512, s)
    eb = 2048
    ne = ne_total // eb
    per_b = s // tm
    nk16 = N_KEYS // BF16_ROWS
    kern = functools.partial(_peer_kernel, eb=eb, ne=ne, tm=tm)
    bblk = pl.BlockSpec((PEER_HEADS, nk16, BF16_ROWS, tm), lambda i, e: (0, 0, 0, i))
    fblk = pl.BlockSpec((PEER_HEADS, N_KEYS, tm), lambda i, e: (0, 0, i))
    xblk = pl.BlockSpec((None, tm, d), lambda i, e: (i // per_b, i % per_b, 0))
    return pl.pallas_call(
        kern,
        grid=(t // tm, ne),
        in_specs=[pl.BlockSpec((d, tm), lambda i, e: (0, i)),
                  pl.BlockSpec((eb, d), lambda i, e: (e, 0)),
                  pl.BlockSpec((d, eb), lambda i, e: (0, e)),
                  bblk, bblk, fblk, fblk,
                  xblk,
                  pl.BlockSpec((None, 1, d), lambda i, e: (i // per_b, 0, 5)),
                  pl.BlockSpec((1, d), lambda i, e: (0, 0))],
        out_specs=xblk,
        out_shape=jax.ShapeDtypeStruct((b, s, d), F32),
        scratch_shapes=[pltpu.VMEM((d, tm), F32), pltpu.VMEM((eb, tm), BF16)],
        compiler_params=_cparams(("parallel", "arbitrary")),
        name="peer",
    )(h2t, u, vt, rank2, e2, cnt, coef, x1, mod3, gf)


def _sincos_2d(num_tokens, dim):
    rows = num_tokens // GRID_W
    row = jnp.repeat(jnp.arange(rows, dtype=F32), GRID_W)
    col = jnp.tile(jnp.arange(GRID_W, dtype=F32), rows)
    nf = dim // 4
    omega = 1.0 / (POS_BASE ** (jnp.arange(nf, dtype=F32) / nf))
    er = row[:, None] * omega[None]
    ec = col[:, None] * omega[None]
    return jnp.concatenate([jnp.sin(er), jnp.cos(er), jnp.sin(ec), jnp.cos(ec)], axis=-1)


def _dft_tables(n):
    k = jnp.arange(n, dtype=jnp.int32)
    ang = ((k[:, None] * k[None, :]) % n).astype(F32) * (2.0 * math.pi / n)
    return jnp.cos(ang), jnp.sin(ang)


def kernel(x, c, ctx, c_ctx, w_mod, b_mod, norm1_g, norm2_g, w_in, w_fourier, conv_w, conv_b,
           lru_wa, lru_ba, lru_wx, lru_bx, lru_lambda, w_lru_out, w_out, b_out,
           peer_wq, peer_keys, peer_u, peer_v, final_g):
    b, s, d = x.shape
    ctx_len = ctx.shape[1]
    f = FOURIER_GROUPS * FOURIER_GROUP_DIM
    r = LRU_HEADS * LRU_HEAD_DIM
    assert w_mod.shape[0] == 1, "single-layer block"
    assert w_in.shape[2] == f + 2 * r + 2 * d

    pos = _sincos_2d(s, d)
    ct, st = _dft_tables(s)
    cc, sc = _dft_tables(FOURIER_GROUP_DIM)
    scale = 1.0 / math.sqrt(s * FOURIER_GROUP_DIM)
    eye = jnp.eye(FOURIER_GROUPS, dtype=F32)
    wc = jnp.concatenate([jnp.kron(eye, cc), jnp.kron(eye, sc)], axis=1) * scale

    rp = -(-(b + 1) // SUBLANES) * SUBLANES
    cond = jnp.concatenate([c, c_ctx[None], jnp.zeros((rp - b - 1, d), F32)], axis=0)
    mod = _adaln(cond, w_mod[0], b_mod)
    mod3 = mod.reshape(rp, 1, 6 * d)

    w_in_b = w_in[0].astype(BF16)
    g1 = norm1_g
    g2 = norm2_g
    wa = lru_wa[0].astype(BF16)
    wx = lru_wx[0].astype(BF16)
    ba, bx, lam = lru_ba[0], lru_bx[0], lru_lambda[0]

    def scans(zx, h0f, h0b):
        hf = _scan(zx, conv_w[0], conv_b, wa[0], ba[0:1], wx[0], bx[0:1], lam[0:1], h0f, False)
        hb = _scan(zx, conv_w[0], conv_b, wa[1], ba[1:2], wx[1], bx[1:2], lam[1:2], h0b, True)
        return hf, hb

    zx_c = _ctx_proj(ctx.reshape(b * ctx_len, d), mod3, b, g1, w_in_b[:, f:f + r])
    zeros = jnp.zeros((b, 1, r), F32)
    hf_c, hb_c = scans(zx_c.reshape(b, ctx_len, r), zeros, zeros)
    htf = hf_c[:, ctx_len - 1:ctx_len, :]
    htb = hb_c[:, 0:1, :]

    ab, zx, gy, sga, sgb = _proj_in(x, pos, mod3, g1, w_in_b, wc.astype(BF16), f, r)
    hf, hb = scans(zx, htf, htb)
    x1, h2t = _mix(x, pos, mod3, g2, hf, hb, gy, sga, sgb, ab,
                   ct.astype(BF16), st.astype(BF16),
                   w_fourier[0].astype(BF16), w_lru_out[0].astype(BF16),
                   w_out[0].astype(BF16), b_out, f)

    keys = peer_keys[0].reshape(2 * PEER_HEADS, N_KEYS, PEER_HALF_DIM)
    rank2, e2, cnt, coef = _route(h2t, _keyproj(keys, peer_wq[0]))
    u = peer_u[0].astype(BF16)
    vt = peer_v[0].T.astype(BF16)
    return _peer(h2t, u, vt, rank2, e2, cnt, coef, x1, mod3, final_g[None])
```

```python
import functools
import math

import jax
import jax.numpy as jnp
from jax import lax
from jax.experimental import pallas as pl
from jax.experimental.pallas import tpu as pltpu

F32 = jnp.float32
BF16 = jnp.bfloat16

NORM_EPS = 1e-6
POS_BASE = 10000.0
GRID_W = 64
FOURIER_GROUPS = 4
FOURIER_GROUP_DIM = 128
LRU_HEADS = 8
LRU_HEAD_DIM = 128
CONV_WIDTH = 4
LRU_C = 8.0
PEER_HEADS = 8
PEER_HALF_DIM = 128
N_KEYS = 128
PEER_TOPK = 16

LANES = 128
SUBLANES = 8
BF16_ROWS = 16
VMEM_LIMIT = 56 * 1024 * 1024

_CELLS = [(r1, r2) for r1 in range(PEER_TOPK) for r2 in range(PEER_TOPK)
          if (r1 + 1) * (r2 + 1) <= PEER_TOPK]


def _cparams(sem):
    return pltpu.CompilerParams(dimension_semantics=sem, vmem_limit_bytes=VMEM_LIMIT)


def _const_spec(shape):
    nd = len(shape)
    return pl.BlockSpec(shape, lambda *_: (0,) * nd, pipeline_mode=pl.Buffered(1))


def _gelu(x):
    c = math.sqrt(2.0 / math.pi)
    return x * (0.5 * (1.0 + jnp.tanh(c * (x + 0.044715 * (x * x * x)))))


def _gelu_x2(x):
    c = math.sqrt(2.0 / math.pi)
    return x * (1.0 + jnp.tanh(x * (c + (c * 0.044715) * (x * x))))


def _norm_mod(x, g, sh, sc):
    ms = jnp.mean(x * x, axis=-1, keepdims=True)
    y = x * lax.rsqrt(ms + NORM_EPS)
    return (y * g) * (1.0 + sc) + sh


def _adaln_kernel(c_ref, w_ref, b_ref, o_ref):
    c = c_ref[...]
    s = c * jax.nn.sigmoid(c)
    o_ref[...] = jnp.dot(s, w_ref[...], preferred_element_type=F32,
                         precision=lax.Precision.HIGHEST) + b_ref[...]


def _adaln(cc, w_mod, b_mod):
    rp, d = cc.shape
    n = w_mod.shape[1]
    tn = 1536
    return pl.pallas_call(
        _adaln_kernel,
        grid=(n // tn,),
        in_specs=[pl.BlockSpec((rp, d), lambda j: (0, 0)),
                  pl.BlockSpec((d, tn), lambda j: (0, j)),
                  pl.BlockSpec((1, tn), lambda j: (0, j))],
        out_specs=pl.BlockSpec((rp, tn), lambda j: (0, j)),
        out_shape=jax.ShapeDtypeStruct((rp, n), F32),
        compiler_params=_cparams(("parallel",)),
        name="adaln",
    )(cc, w_mod, b_mod)


def _ctx_proj_kernel(x_ref, sh_ref, sc_ref, g_ref, w_ref, o_ref):
    h = _norm_mod(x_ref[...], g_ref[...], sh_ref[...], sc_ref[...])
    o_ref[...] = jnp.dot(h.astype(BF16), w_ref[...], preferred_element_type=F32)


def _ctx_proj(ctx2, mod3, row, g1, w_zx):
    n, d = ctx2.shape
    r = w_zx.shape[1]
    tt = min(512, n)
    return pl.pallas_call(
        _ctx_proj_kernel,
        grid=(n // tt,),
        in_specs=[pl.BlockSpec((tt, d), lambda i: (i, 0)),
                  pl.BlockSpec((None, 1, d), lambda i: (row, 0, 0)),
                  pl.BlockSpec((None, 1, d), lambda i: (row, 0, 1)),
                  pl.BlockSpec((1, d), lambda i: (0, 0)),
                  _const_spec((d, r))],
        out_specs=pl.BlockSpec((tt, r), lambda i: (i, 0)),
        out_shape=jax.ShapeDtypeStruct((n, r), F32),
        compiler_params=_cparams(("parallel",)),
        name="ctx_proj",
    )(ctx2, mod3, mod3, g1, w_zx)


def _scan_kernel(zc_ref, zp_ref, zn_ref, cw_ref, cb_ref, wa_ref, ba_ref, wx_ref, bx_ref,
                 lam_ref, h0_ref, o_ref, a_scr, b_scr, h_scr, *, reverse, nt, ts):
    j = pl.program_id(1)
    jj = (nt - 1 - j) if reverse else j

    @pl.when(j == 0)
    def _():
        h_scr[...] = h0_ref[...]

    cur = zc_ref[...]
    has_prev = (jj > 0).astype(F32)
    has_next = (jj < nt - 1).astype(F32)
    prev_row = zp_ref[SUBLANES - 1:SUBLANES, :] * has_prev
    next0 = zn_ref[0:1, :] * has_next
    next1 = zn_ref[1:2, :] * has_next
    row = lax.broadcasted_iota(jnp.int32, cur.shape, 0)
    zm1 = jnp.where(row == 0, prev_row, pltpu.roll(cur, 1, axis=0))
    zp1 = jnp.where(row == ts - 1, next0, pltpu.roll(cur, ts - 1, axis=0))
    zp2 = pltpu.roll(cur, ts - 2, axis=0)
    zp2 = jnp.where(row == ts - 2, next0, jnp.where(row == ts - 1, next1, zp2))
    xc = cb_ref[...] + zm1 * cw_ref[0:1, :]
    xc = xc + cur * cw_ref[1:2, :]
    xc = xc + zp1 * cw_ref[2:3, :]
    xc = xc + zp2 * cw_ref[3:4, :]

    lam = lam_ref[...]
    softplus_neg = jnp.maximum(-lam, 0.0) + jnp.log1p(jnp.exp(-jnp.abs(lam)))
    half_c = (0.5 * LRU_C) * softplus_neg
    for hd in range(LRU_HEADS):
        cols = slice(hd * LRU_HEAD_DIM, (hd + 1) * LRU_HEAD_DIM)
        xh = xc[:, cols]
        xb = xh.astype(BF16)
        hzr = jnp.dot(xb, wa_ref[hd], preferred_element_type=F32) + ba_ref[:, cols]
        hzi = jnp.dot(xb, wx_ref[hd], preferred_element_type=F32) + bx_ref[:, cols]
        neg_log_a = half_c[:, cols] * jnp.tanh(hzr) + half_c[:, cols]
        a = jnp.exp(-neg_log_a)
        a_scr[:, cols] = a
        y = jnp.tanh(neg_log_a) * (a * a + 1.0)
        root = jnp.where(y > 0.0, y * lax.rsqrt(y), 0.0)
        b_scr[:, cols] = root * ((0.5 * jnp.tanh(hzi) + 0.5) * xh)

    def step(k, h):
        t = (ts - 1 - k) if reverse else k
        h = a_scr[pl.ds(t, 1), :] * h + b_scr[pl.ds(t, 1), :]
        o_ref[pl.ds(t, 1), :] = h
        return h

    h_scr[...] = lax.fori_loop(0, ts, step, h_scr[...], unroll=8)


def _scan(zx, conv_w, conv_b, wa, ba, wx, bx, lam, h0, reverse):
    b, s, r = zx.shape
    ts = min(256, s)
    nt = s // ts
    nb8 = s // SUBLANES
    per = ts // SUBLANES

    def jj(j):
        return (nt - 1 - j) if reverse else j

    kern = functools.partial(_scan_kernel, reverse=reverse, nt=nt, ts=ts)
    return pl.pallas_call(
        kern,
        grid=(b, nt),
        in_specs=[pl.BlockSpec((None, ts, r), lambda i, j: (i, jj(j), 0)),
                  pl.BlockSpec((None, SUBLANES, r),
                               lambda i, j: (i, jnp.maximum(jj(j) * per - 1, 0), 0)),
                  pl.BlockSpec((None, SUBLANES, r),
                               lambda i, j: (i, jnp.minimum((jj(j) + 1) * per, nb8 - 1), 0)),
                  _const_spec((CONV_WIDTH, r)),
                  _const_spec((1, r)),
                  _const_spec((LRU_HEADS, LRU_HEAD_DIM, LRU_HEAD_DIM)),
                  _const_spec((1, r)),
                  _const_spec((LRU_HEADS, LRU_HEAD_DIM, LRU_HEAD_DIM)),
                  _const_spec((1, r)),
                  _const_spec((1, r)),
                  pl.BlockSpec((None, 1, r), lambda i, j: (i, 0, 0))],
        out_specs=pl.BlockSpec((None, ts, r), lambda i, j: (i, jj(j), 0)),
        out_shape=jax.ShapeDtypeStruct((b, s, r), F32),
        scratch_shapes=[pltpu.VMEM((ts, r), F32), pltpu.VMEM((ts, r), F32),
                        pltpu.VMEM((1, r), F32)],
        compiler_params=_cparams(("arbitrary", "arbitrary")),
        name="scan_bwd" if reverse else "scan_fwd",
    )(zx, zx, zx, conv_w, conv_b, wa, ba, wx, bx, lam, h0)


def _proj_in_kernel(x_ref, pos_ref, sh_ref, sc_ref, g_ref, w_ref, wc_ref,
                    ab_ref, zx_ref, gy_ref, sga_ref, sgb_ref, *, f, r, d):
    xp = x_ref[...] + pos_ref[...]
    h = _norm_mod(xp, g_ref[...], sh_ref[...], sc_ref[...]).astype(BF16)
    zf = jnp.dot(h, w_ref[:, 0:f], preferred_element_type=F32)
    ab_ref[...] = jnp.dot(zf.astype(BF16), wc_ref[...], preferred_element_type=F32).astype(BF16)
    zx_ref[...] = jnp.dot(h, w_ref[:, f:f + r], preferred_element_type=F32)
    zy = jnp.dot(h, w_ref[:, f + r:f + 2 * r], preferred_element_type=F32)
    gy_ref[...] = _gelu(zy).astype(BF16)
    zga = jnp.dot(h, w_ref[:, f + 2 * r:f + 2 * r + d], preferred_element_type=F32)
    sga_ref[...] = jax.nn.sigmoid(zga).astype(BF16)
    zgb = jnp.dot(h, w_ref[:, f + 2 * r + d:f + 2 * r + 2 * d], preferred_element_type=F32)
    sgb_ref[...] = jax.nn.sigmoid(zgb).astype(BF16)


def _proj_in(x, pos, mod3, g1, w_in, wc, f, r):
    b, s, d = x.shape
    tt = min(512, s)
    nt = s // tt
    kern = functools.partial(_proj_in_kernel, f=f, r=r, d=d)
    blk = lambda w: pl.BlockSpec((None, tt, w), lambda j, i: (i, j, 0))
    return pl.pallas_call(
        kern,
        grid=(nt, b),
        in_specs=[blk(d),
                  pl.BlockSpec((tt, d), lambda j, i: (j, 0)),
                  pl.BlockSpec((None, 1, d), lambda j, i: (i, 0, 0)),
                  pl.BlockSpec((None, 1, d), lambda j, i: (i, 0, 1)),
                  pl.BlockSpec((1, d), lambda j, i: (0, 0)),
                  _const_spec(w_in.shape),
                  _const_spec(wc.shape)],
        out_specs=[blk(2 * f), blk(r), blk(r), blk(d), blk(d)],
        out_shape=[jax.ShapeDtypeStruct((b, s, 2 * f), BF16),
                   jax.ShapeDtypeStruct((b, s, r), F32),
                   jax.ShapeDtypeStruct((b, s, r), BF16),
                   jax.ShapeDtypeStruct((b, s, d), BF16),
                   jax.ShapeDtypeStruct((b, s, d), BF16)],
        compiler_params=_cparams(("parallel", "parallel")),
        name="proj_in",
    )(x, pos, mod3, mod3, g1, w_in, wc)


def _mix_kernel(x_ref, pos_ref, g1m_ref, sh2_ref, sc2_ref, g2n_ref, hf_ref, hb_ref, gy_ref,
                sga_ref, sgb_ref, ab_ref, ct_ref, st_ref, wf_ref, wl_ref, wo_ref, bo_ref,
                x1_ref, h2t_ref, *, f, tm):
    rows = pl.ds(pl.multiple_of(pl.program_id(1) * tm, tm), tm)
    yf = jnp.dot(ct_ref[rows, :], ab_ref[:, 0:f], preferred_element_type=F32)
    yf = yf - jnp.dot(st_ref[rows, :], ab_ref[:, f:2 * f], preferred_element_type=F32)
    ya = jnp.dot(yf.astype(BF16), wf_ref[...], preferred_element_type=F32)
    hs = (hf_ref[...] + hb_ref[...]) * gy_ref[...].astype(F32)
    yb = jnp.dot(hs.astype(BF16), wl_ref[...], preferred_element_type=F32)
    merged = sga_ref[...].astype(F32) * ya + sgb_ref[...].astype(F32) * yb
    mix = jnp.dot(merged.astype(BF16), wo_ref[...], preferred_element_type=F32) + bo_ref[...]
    x1 = (x_ref[...] + pos_ref[...]) + g1m_ref[...] * mix
    x1_ref[...] = x1
    h2 = _norm_mod(x1, g2n_ref[...], sh2_ref[...], sc2_ref[...])
    h2t_ref[...] = h2.T.astype(BF16)


def _mix(x, pos, mod3, g2n, hf, hb, gy, sga, sgb, ab, ct, st, wf, wl, wo, bo, f):
    b, s, d = x.shape
    r = hf.shape[-1]
    tm = min(256, s)
    nt = s // tm
    kern = functools.partial(_mix_kernel, f=f, tm=tm)
    blk = lambda w: pl.BlockSpec((None, tm, w), lambda i, j: (i, j, 0))
    modrow = lambda k: pl.BlockSpec((None, 1, d), lambda i, j: (i, 0, k))
    return pl.pallas_call(
        kern,
        grid=(b, nt),
        in_specs=[blk(d),
                  pl.BlockSpec((tm, d), lambda i, j: (j, 0)),
                  modrow(2), modrow(3), modrow(4),
                  pl.BlockSpec((1, d), lambda i, j: (0, 0)),
                  blk(r), blk(r), blk(r), blk(d), blk(d),
                  pl.BlockSpec((None, s, 2 * f), lambda i, j: (i, 0, 0),
                               pipeline_mode=pl.Buffered(1)),
                  _const_spec(ct.shape), _const_spec(st.shape),
                  _const_spec(wf.shape), _const_spec(wl.shape), _const_spec(wo.shape),
                  _const_spec(bo.shape)],
        out_specs=[blk(d),
                   pl.BlockSpec((d, tm), lambda i, j: (0, i * nt + j))],
        out_shape=[jax.ShapeDtypeStruct((b, s, d), F32),
                   jax.ShapeDtypeStruct((d, b * s), BF16)],
        compiler_params=_cparams(("parallel", "arbitrary")),
        name="mix",
    )(x, pos, mod3, mod3, mod3, g2n, hf, hb, gy, sga, sgb, ab, ct, st, wf, wl, wo, bo)


def _batcher_pairs(n):
    pairs = []

    def merge(lo, m, r):
        step = r * 2
        if step < m:
            merge(lo, m, step)
            merge(lo + r, m, step)
            pairs.extend((i, i + r) for i in range(lo + r, lo + m - r, step))
        else:
            pairs.append((lo, lo + r))

    def sort(lo, m):
        if m > 1:
            sort(lo, m // 2)
            sort(lo + m // 2, m // 2)
            merge(lo, m, 1)

    sort(0, n)
    return pairs


_SORT16 = _batcher_pairs(PEER_TOPK)


def _exchange(a, i, j):
    hi, lo = jnp.maximum(a[i], a[j]), jnp.minimum(a[i], a[j])
    a[i], a[j] = hi, lo


def _sorted_top16(tiles):
    a = list(tiles)
    for i, j in _SORT16:
        _exchange(a, i, j)
    for shift in (4, 2, 1):
        b = [pltpu.roll(x, shift, axis=0) for x in a]
        a = [jnp.maximum(a[v], b[PEER_TOPK - 1 - v]) for v in range(PEER_TOPK)]
        for stride in (8, 4, 2, 1):
            for i in range(PEER_TOPK):
                if not i & stride:
                    _exchange(a, i, i + stride)
    return a


def _top16(s, kiota):
    rank = jnp.full(s.shape, float(PEER_TOPK), F32)
    for r in range(PEER_TOPK):
        m = jnp.max(s, axis=0, keepdims=True)
        first = jnp.min(jnp.where(s == m, kiota, float(N_KEYS)), axis=0, keepdims=True)
        hit = kiota == first
        s = jnp.where(hit, -jnp.inf, s)
        rank = jnp.where(hit, float(r), rank)
    return rank


def _keyproj_kernel(keys_ref, wq_ref, o_ref):
    o_ref[...] = lax.dot_general(keys_ref[...], wq_ref[...], (((1,), (1,)), ((), ())),
                                 preferred_element_type=F32,
                                 precision=lax.Precision.HIGHEST).astype(BF16)


def _keyproj(keys, wq):
    nhp, nk, hd = keys.shape
    d = wq.shape[0]
    return pl.pallas_call(
        _keyproj_kernel,
        grid=(nhp,),
        in_specs=[pl.BlockSpec((None, nk, hd), lambda i: (i, 0, 0)),
                  pl.BlockSpec((d, hd), lambda i: (0, i))],
        out_specs=pl.BlockSpec((nk, d), lambda i: (i, 0)),
        out_shape=jax.ShapeDtypeStruct((nhp * nk, d), BF16),
        compiler_params=_cparams(("parallel",)),
        name="keyproj",
    )(keys, wq)


def _route_kernel(h2t_ref, kw_ref, rank2_ref, e2_ref, cnt_ref, coef_ref,
                  s_scr, sv_scr, c_scr, z_scr, flag_scr, *, tr):
    nh = PEER_HEADS
    ntile = N_KEYS // SUBLANES
    nk16 = N_KEYS // BF16_ROWS
    s_all = jnp.dot(kw_ref[...], h2t_ref[...], preferred_element_type=F32)
    for hp in range(2 * nh):
        s_scr[hp] = s_all[hp * N_KEYS:(hp + 1) * N_KEYS, :]

    kiota = lax.broadcasted_iota(jnp.int32, (N_KEYS, LANES), 0).astype(F32)

    def lane_block(lb, _):
        l0 = pl.multiple_of(lb * LANES, LANES)
        lanes = pl.ds(l0, LANES)

        def sort_head(h, _):
            hrow = pl.ds(h, 1)
            for p in range(2):
                tiles = [s_scr[2 * h + p, v * SUBLANES:(v + 1) * SUBLANES, lanes]
                         for v in range(ntile)]
                sv = _sorted_top16(tiles)
                for r in range(PEER_TOPK):
                    sv_scr[p, r, hrow, :] = sv[r][0:1, :]
                nsel = jnp.zeros((SUBLANES, LANES), F32)
                for v in range(ntile):
                    nsel = nsel + jnp.where(tiles[v] >= sv[PEER_TOPK - 1], 1.0, 0.0)
                nsel = jnp.sum(nsel, axis=0, keepdims=True)
                flag_scr[p, hrow, :] = jnp.where(nsel != float(PEER_TOPK), 1.0, 0.0)
                if p == 1:
                    ranks = []
                    for v in range(ntile):
                        rank = jnp.zeros((SUBLANES, LANES), F32)
                        for r in range(PEER_TOPK):
                            rank = jnp.where(sv[r] > tiles[v], float(r + 1), rank)
                        ranks.append(rank)
                    for k in range(nk16):
                        pair = jnp.concatenate([ranks[2 * k], ranks[2 * k + 1]], axis=0)
                        rank2_ref[h, k, :, lanes] = pair.astype(BF16)
            return 0

        lax.fori_loop(0, nh, sort_head, 0)

        sv1 = [sv_scr[0, r] for r in range(PEER_TOPK)]
        sv2 = [sv_scr[1, r] for r in range(PEER_TOPK)]
        val = {c: sv1[c[0]] + sv2[c[1]] for c in _CELLS}
        before = {c: 0.0 for c in _CELLS}
        for ia, ca in enumerate(_CELLS):
            for cb in _CELLS[ia + 1:]:
                if ca[0] <= cb[0] and ca[1] <= cb[1]:
                    before[cb] = before[cb] + 1.0
                elif cb[0] <= ca[0] and cb[1] <= ca[1]:
                    before[ca] = before[ca] + 1.0
                else:
                    ge = jnp.where(val[ca] >= val[cb], 1.0, 0.0)
                    before[cb] = before[cb] + ge
                    before[ca] = before[ca] + (1.0 - ge)
        ex1 = [jnp.exp(sv1[r] - sv1[0]) for r in range(PEER_TOPK)]
        ex2 = [jnp.exp(sv2[r] - sv2[0]) for r in range(PEER_TOPK)]
        zsum = jnp.zeros((nh, LANES), F32)
        cnt = [jnp.zeros((nh, LANES), F32) for _ in range(PEER_TOPK)]
        for c in _CELLS:
            sel = before[c] < float(PEER_TOPK)
            cnt[c[0]] = cnt[c[0]] + jnp.where(sel, 1.0, 0.0)
            zsum = zsum + jnp.where(sel, ex1[c[0]] * ex2[c[1]], 0.0)
        for r in range(PEER_TOPK):
            c_scr[r] = cnt[r]
        z_scr[0] = 0.5 / zsum
        z_scr[1] = sv1[0]
        z_scr[2] = sv2[0]
        tie = jnp.maximum(flag_scr[0], flag_scr[1])
        for r in range(PEER_TOPK - 1):
            tie = jnp.where(sv1[r] == sv1[r + 1], 1.0, tie)
            tie = jnp.where(sv2[r] == sv2[r + 1], 1.0, tie)
        flag_scr[2] = tie

        def emit_head(h, _):
            hrow = pl.ds(h, 1)
            tile = lambda ref, *idx: jnp.broadcast_to(ref[(*idx, hrow, slice(None))],
                                                      (SUBLANES, LANES))
            sv1b = [tile(sv_scr, 0, r) for r in range(PEER_TOPK)]
            cb = [tile(c_scr, r) for r in range(PEER_TOPK)]
            gate, max1, max2 = tile(z_scr, 0), tile(z_scr, 1), tile(z_scr, 2)
            e2 = []
            for v in range(ntile):
                rows = slice(v * SUBLANES, (v + 1) * SUBLANES)
                s1 = s_scr[2 * h, rows, lanes]
                cnt1 = jnp.zeros((SUBLANES, LANES), F32)
                for r in range(PEER_TOPK):
                    cnt1 = jnp.where(s1 == sv1b[r], cb[r], cnt1)
                cnt_ref[h, rows, lanes] = cnt1
                coef_ref[h, rows, lanes] = jnp.exp(s1 - max1) * gate
                e2.append(jnp.exp(s_scr[2 * h + 1, rows, lanes] - max2))
            for k in range(nk16):
                pair = jnp.concatenate([e2[2 * k], e2[2 * k + 1]], axis=0)
                e2_ref[h, k, :, lanes] = pair.astype(BF16)
            return 0

        lax.fori_loop(0, nh, emit_head, 0)

        @pl.when(jnp.max(tie) > 0.0)
        def _():
            def fix_head(h, _):
                hrow = pl.ds(h, 1)

                @pl.when(jnp.max(flag_scr[2, hrow, :]) > 0.0)
                def _():
                    rank1 = _top16(s_scr[2 * h, :, lanes], kiota)
                    cnt1 = jnp.zeros((N_KEYS, LANES), F32)
                    for r in range(PEER_TOPK):
                        cnt1 = jnp.where(rank1 == float(r), c_scr[r, hrow, :], cnt1)
                    cnt_ref[h, :, lanes] = cnt1
                    rank2 = _top16(s_scr[2 * h + 1, :, lanes], kiota)
                    for k in range(nk16):
                        rows = slice(k * BF16_ROWS, (k + 1) * BF16_ROWS)
                        rank2_ref[h, k, :, lanes] = rank2[rows].astype(BF16)

                return 0

            lax.fori_loop(0, nh, fix_head, 0)

        return 0

    lax.fori_loop(0, tr // LANES, lane_block, 0)


def _route(h2t, kw):
    d, t = h2t.shape
    nh = PEER_HEADS
    tr = 256
    nk16 = N_KEYS // BF16_ROWS
    kern = functools.partial(_route_kernel, tr=tr)
    bblk = pl.BlockSpec((nh, nk16, BF16_ROWS, tr), lambda i: (0, 0, 0, i))
    fblk = pl.BlockSpec((nh, N_KEYS, tr), lambda i: (0, 0, i))
    return pl.pallas_call(
        kern,
        grid=(t // tr,),
        in_specs=[pl.BlockSpec((d, tr), lambda i: (0, i)),
                  _const_spec(kw.shape)],
        out_specs=[bblk, bblk, fblk, fblk],
        out_shape=[jax.ShapeDtypeStruct((nh, nk16, BF16_ROWS, t), BF16),
                   jax.ShapeDtypeStruct((nh, nk16, BF16_ROWS, t), BF16),
                   jax.ShapeDtypeStruct((nh, N_KEYS, t), F32),
                   jax.ShapeDtypeStruct((nh, N_KEYS, t), F32)],
        scratch_shapes=[pltpu.VMEM((2 * nh, N_KEYS, tr), F32),
                        pltpu.VMEM((2, PEER_TOPK, nh, LANES), F32),
                        pltpu.VMEM((PEER_TOPK, nh, LANES), F32),
                        pltpu.VMEM((3, nh, LANES), F32),
                        pltpu.VMEM((3, nh, LANES), F32)],
        compiler_params=_cparams(("parallel",)),
        name="route",
    )(h2t, kw)


def _peer_kernel(h2t_ref, u_ref, vt_ref, rank2_ref, e2_ref, cnt_ref, coef_ref, x1_ref,
                 g2m_ref, gf_ref, o_ref, acc_ref, wt_ref, *, eb, ne, tm):
    e = pl.program_id(1)
    nk16 = N_KEYS // BF16_ROWS

    @pl.when(e == 0)
    def _():
        acc_ref[...] = jnp.zeros_like(acc_ref)

    st = jnp.dot(u_ref[...], h2t_ref[...], preferred_element_type=F32)
    for k in range(eb // N_KEYS):
        i1 = e * (eb // N_KEYS) + k
        cnt = [jnp.broadcast_to(cnt_ref[h, pl.ds(i1, 1), :], (BF16_ROWS, tm)).astype(BF16)
               for h in range(PEER_HEADS)]
        cf = [jnp.broadcast_to(coef_ref[h, pl.ds(i1, 1), :], (BF16_ROWS, tm)).astype(BF16)
              for h in range(PEER_HEADS)]
        for c in range(nk16):
            m = None
            for h in range(PEER_HEADS):
                term = jnp.where(rank2_ref[h, c] < cnt[h], e2_ref[h, c] * cf[h],
                                 jnp.zeros((), BF16))
                m = term if m is None else m + term
            r0 = k * N_KEYS + c * BF16_ROWS
            act2 = _gelu_x2(st[r0:r0 + BF16_ROWS, :].astype(BF16))
            wt_ref[r0:r0 + BF16_ROWS, :] = m * act2
    acc_ref[...] += jnp.dot(vt_ref[...], wt_ref[...], preferred_element_type=F32)

    @pl.when(e == ne - 1)
    def _():
        x2 = x1_ref[...] + g2m_ref[...] * acc_ref[...].T
        ms = jnp.mean(x2 * x2, axis=-1, keepdims=True)
        o_ref[...] = (x2 * lax.rsqrt(ms + NORM_EPS)) * gf_ref[...]


def _peer(h2t, u, vt, rank2, e2, cnt, coef, x1, mod3, gf):
    b, s, d = x1.shape
    t = b * s
    ne_total = u.shape[0]
    tm = min(512, s)
    eb = 2048
    ne = ne_total // eb
    per_b = s // tm
    nk16 = N_KEYS // BF16_ROWS
    kern = functools.partial(_peer_kernel, eb=eb, ne=ne, tm=tm)
    bblk = pl.BlockSpec((PEER_HEADS, nk16, BF16_ROWS, tm), lambda i, e: (0, 0, 0, i))
    fblk = pl.BlockSpec((PEER_HEADS, N_KEYS, tm), lambda i, e: (0, 0, i))
    xblk = pl.BlockSpec((None, tm, d), lambda i, e: (i // per_b, i % per_b, 0))
    return pl.pallas_call(
        kern,
        grid=(t // tm, ne),
        in_specs=[pl.BlockSpec((d, tm), lambda i, e: (0, i)),
                  pl.BlockSpec((eb, d), lambda i, e: (e, 0)),
                  pl.BlockSpec((d, eb), lambda i, e: (0, e)),
                  bblk, bblk, fblk, fblk,
                  xblk,
                  pl.BlockSpec((None, 1, d), lambda i, e: (i // per_b, 0, 5)),
                  pl.BlockSpec((1, d), lambda i, e: (0, 0))],
        out_specs=xblk,
        out_shape=jax.ShapeDtypeStruct((b, s, d), F32),
        scratch_shapes=[pltpu.VMEM((d, tm), F32), pltpu.VMEM((eb, tm), BF16)],
        compiler_params=_cparams(("parallel", "arbitrary")),
        name="peer",
    )(h2t, u, vt, rank2, e2, cnt, coef, x1, mod3, gf)


def _sincos_2d(num_tokens, dim):
    rows = num_tokens // GRID_W
    row = jnp.repeat(jnp.arange(rows, dtype=F32), GRID_W)
    col = jnp.tile(jnp.arange(GRID_W, dtype=F32), rows)
    nf = dim // 4
    omega = 1.0 / (POS_BASE ** (jnp.arange(nf, dtype=F32) / nf))
    er = row[:, None] * omega[None]
    ec = col[:, None] * omega[None]
    return jnp.concatenate([jnp.sin(er), jnp.cos(er), jnp.sin(ec), jnp.cos(ec)], axis=-1)


def _dft_tables(n):
    k = jnp.arange(n, dtype=jnp.int32)
    ang = ((k[:, None] * k[None, :]) % n).astype(F32) * (2.0 * math.pi / n)
    return jnp.cos(ang), jnp.sin(ang)


def kernel(x, c, ctx, c_ctx, w_mod, b_mod, norm1_g, norm2_g, w_in, w_fourier, conv_w, conv_b,
           lru_wa, lru_ba, lru_wx, lru_bx, lru_lambda, w_lru_out, w_out, b_out,
           peer_wq, peer_keys, peer_u, peer_v, final_g):
    b, s, d = x.shape
    ctx_len = ctx.shape[1]
    f = FOURIER_GROUPS * FOURIER_GROUP_DIM
    r = LRU_HEADS * LRU_HEAD_DIM
    assert w_mod.shape[0] == 1, "single-layer block"
    assert w_in.shape[2] == f + 2 * r + 2 * d

    pos = _sincos_2d(s, d)
    ct, st = _dft_tables(s)
    cc, sc = _dft_tables(FOURIER_GROUP_DIM)
    scale = 1.0 / math.sqrt(s * FOURIER_GROUP_DIM)
    eye = jnp.eye(FOURIER_GROUPS, dtype=F32)
    wc = jnp.concatenate([jnp.kron(eye, cc), jnp.kron(eye, sc)], axis=1) * scale

    rp = -(-(b + 1) // SUBLANES) * SUBLANES
    cond = jnp.concatenate([c, c_ctx[None], jnp.zeros((rp - b - 1, d), F32)], axis=0)
    mod = _adaln(cond, w_mod[0], b_mod)
    mod3 = mod.reshape(rp, 1, 6 * d)

    w_in_b = w_in[0].astype(BF16)
    g1 = norm1_g
    g2 = norm2_g
    wa = (0.5 * lru_wa[0]).astype(BF16)
    wx = (0.5 * lru_wx[0]).astype(BF16)
    ba, bx, lam = 0.5 * lru_ba[0], 0.5 * lru_bx[0], lru_lambda[0]

    def scans(zx, h0f, h0b):
        hf = _scan(zx, conv_w[0], conv_b, wa[0], ba[0:1], wx[0], bx[0:1], lam[0:1], h0f, False)
        hb = _scan(zx, conv_w[0], conv_b, wa[1], ba[1:2], wx[1], bx[1:2], lam[1:2], h0b, True)
        return hf, hb

    zx_c = _ctx_proj(ctx.reshape(b * ctx_len, d), mod3, b, g1, w_in_b[:, f:f + r])
    zeros = jnp.zeros((b, 1, r), F32)
    hf_c, hb_c = scans(zx_c.reshape(b, ctx_len, r), zeros, zeros)
    htf = hf_c[:, ctx_len - 1:ctx_len, :]
    htb = hb_c[:, 0:1, :]

    ab, zx, gy, sga, sgb = _proj_in(x, pos, mod3, g1, w_in_b, wc.astype(BF16), f, r)
    hf, hb = scans(zx, htf, htb)
    x1, h2t = _mix(x, pos, mod3, g2, hf, hb, gy, sga, sgb, ab,
                   ct.astype(BF16), st.astype(BF16),
                   w_fourier[0].astype(BF16), w_lru_out[0].astype(BF16),
                   w_out[0].astype(BF16), b_out, f)

    keys = peer_keys[0].reshape(2 * PEER_HEADS, N_KEYS, PEER_HALF_DIM)
    rank2, e2, cnt, coef = _route(h2t, _keyproj(keys, peer_wq[0]))
    u = peer_u[0].astype(BF16)
    vt = peer_v[0].T.astype(BF16)
    return _peer(h2t, u, vt, rank2, e2, cnt, coef, x1, mod3, final_g[None])
```

```python
import functools
import math

import jax
import jax.numpy as jnp
from jax import lax
from jax.experimental import pallas as pl
from jax.experimental.pallas import tpu as pltpu

F32 = jnp.float32
BF16 = jnp.bfloat16

NORM_EPS = 1e-6
POS_BASE = 10000.0
GRID_W = 64
FOURIER_GROUPS = 4
FOURIER_GROUP_DIM = 128
LRU_HEADS = 8
LRU_HEAD_DIM = 128
CONV_WIDTH = 4
LRU_C = 8.0
PEER_HEADS = 8
PEER_HALF_DIM = 128
N_KEYS = 128
PEER_TOPK = 16

LANES = 128
SUBLANES = 8
BF16_ROWS = 16
VMEM_LIMIT = 56 * 1024 * 1024

_CELLS = [(r1, r2) for r1 in range(PEER_TOPK) for r2 in range(PEER_TOPK)
          if (r1 + 1) * (r2 + 1) <= PEER_TOPK]


def _cparams(sem):
    return pltpu.CompilerParams(dimension_semantics=sem, vmem_limit_bytes=VMEM_LIMIT)


def _const_spec(shape):
    nd = len(shape)
    return pl.BlockSpec(shape, lambda *_: (0,) * nd, pipeline_mode=pl.Buffered(1))


def _gelu(x):
    c = math.sqrt(2.0 / math.pi)
    return x * (0.5 * (1.0 + jnp.tanh(c * (x + 0.044715 * (x * x * x)))))


def _gelu_x2(x):
    c = math.sqrt(2.0 / math.pi)
    return x * (1.0 + jnp.tanh(x * (c + (c * 0.044715) * (x * x))))


def _norm_mod(x, g, sh, sc):
    ms = jnp.mean(x * x, axis=-1, keepdims=True)
    y = x * lax.rsqrt(ms + NORM_EPS)
    return (y * g) * (1.0 + sc) + sh


def _adaln_kernel(c_ref, w_ref, b_ref, o_ref):
    c = c_ref[...]
    s = c * jax.nn.sigmoid(c)
    o_ref[...] = jnp.dot(s, w_ref[...], preferred_element_type=F32,
                         precision=lax.Precision.HIGHEST) + b_ref[...]


def _adaln(cc, w_mod, b_mod):
    rp, d = cc.shape
    n = w_mod.shape[1]
    tn = 1536
    return pl.pallas_call(
        _adaln_kernel,
        grid=(n // tn,),
        in_specs=[pl.BlockSpec((rp, d), lambda j: (0, 0)),
                  pl.BlockSpec((d, tn), lambda j: (0, j)),
                  pl.BlockSpec((1, tn), lambda j: (0, j))],
        out_specs=pl.BlockSpec((rp, tn), lambda j: (0, j)),
        out_shape=jax.ShapeDtypeStruct((rp, n), F32),
        compiler_params=_cparams(("parallel",)),
        name="adaln",
    )(cc, w_mod, b_mod)


def _ctx_proj_kernel(x_ref, sh_ref, sc_ref, g_ref, w_ref, o_ref):
    h = _norm_mod(x_ref[...], g_ref[...], sh_ref[...], sc_ref[...])
    o_ref[...] = jnp.dot(h.astype(BF16), w_ref[...], preferred_element_type=F32)


def _ctx_proj(ctx2, mod3, row, g1, w_zx):
    n, d = ctx2.shape
    r = w_zx.shape[1]
    tt = min(512, n)
    return pl.pallas_call(
        _ctx_proj_kernel,
        grid=(n // tt,),
        in_specs=[pl.BlockSpec((tt, d), lambda i: (i, 0)),
                  pl.BlockSpec((None, 1, d), lambda i: (row, 0, 0)),
                  pl.BlockSpec((None, 1, d), lambda i: (row, 0, 1)),
                  pl.BlockSpec((1, d), lambda i: (0, 0)),
                  _const_spec((d, r))],
        out_specs=pl.BlockSpec((tt, r), lambda i: (i, 0)),
        out_shape=jax.ShapeDtypeStruct((n, r), F32),
        compiler_params=_cparams(("parallel",)),
        name="ctx_proj",
    )(ctx2, mod3, mod3, g1, w_zx)


def _scan_kernel(zc_ref, zp_ref, zn_ref, cw_ref, cb_ref, wa_ref, ba_ref, wx_ref, bx_ref,
                 lam_ref, h0_ref, o_ref, a_scr, b_scr, xc_scr, h_scr, *, reverse, nt, ts):
    j = pl.program_id(1)
    jj = (nt - 1 - j) if reverse else j

    @pl.when(j == 0)
    def _():
        h_scr[...] = h0_ref[...]

    cur = zc_ref[...]
    has_prev = (jj > 0).astype(F32)
    has_next = (jj < nt - 1).astype(F32)
    prev_row = zp_ref[SUBLANES - 1:SUBLANES, :] * has_prev
    next0 = zn_ref[0:1, :] * has_next
    next1 = zn_ref[1:2, :] * has_next
    w = [cw_ref[k:k + 1, :] for k in range(CONV_WIDTH)]

    def conv(zm1, z0, zp1, zp2):
        return (((cb_ref[...] + zm1 * w[0]) + z0 * w[1]) + zp1 * w[2]) + zp2 * w[3]

    xc_scr[...] = conv(pltpu.roll(cur, 1, axis=0), cur, pltpu.roll(cur, ts - 1, axis=0),
                       pltpu.roll(cur, ts - 2, axis=0))
    z = lambda i: zc_ref[i:i + 1, :]
    xc_scr[0:1, :] = conv(prev_row, z(0), z(1), z(2))
    xc_scr[ts - 2:ts - 1, :] = conv(z(ts - 3), z(ts - 2), z(ts - 1), next0)
    xc_scr[ts - 1:ts, :] = conv(z(ts - 2), z(ts - 1), next0, next1)

    lam = lam_ref[...]
    softplus_neg = jnp.maximum(-lam, 0.0) + jnp.log1p(jnp.exp(-jnp.abs(lam)))
    half_c = (0.5 * LRU_C) * softplus_neg
    for hd in range(LRU_HEADS):
        cols = slice(hd * LRU_HEAD_DIM, (hd + 1) * LRU_HEAD_DIM)
        xh = xc_scr[:, cols]
        xb = xh.astype(BF16)
        hzr = jnp.dot(xb, wa_ref[hd], preferred_element_type=F32) + ba_ref[:, cols]
        hzi = jnp.dot(xb, wx_ref[hd], preferred_element_type=F32) + bx_ref[:, cols]
        neg_log_a = half_c[:, cols] * jnp.tanh(hzr) + half_c[:, cols]
        a = jnp.exp(-neg_log_a)
        a_scr[:, cols] = a
        y = jnp.tanh(neg_log_a) * (a * a + 1.0)
        root = jnp.where(y > 0.0, y * lax.rsqrt(y), 0.0)
        b_scr[:, cols] = root * ((0.5 * jnp.tanh(hzi) + 0.5) * xh)

    def step(k, h):
        t = (ts - 1 - k) if reverse else k
        h = a_scr[pl.ds(t, 1), :] * h + b_scr[pl.ds(t, 1), :]
        o_ref[pl.ds(t, 1), :] = h
        return h

    h_scr[...] = lax.fori_loop(0, ts, step, h_scr[...], unroll=8)


def _scan(zx, conv_w, conv_b, wa, ba, wx, bx, lam, h0, reverse):
    b, s, r = zx.shape
    ts = min(512, s)
    nt = s // ts
    nb8 = s // SUBLANES
    per = ts // SUBLANES

    def jj(j):
        return (nt - 1 - j) if reverse else j

    kern = functools.partial(_scan_kernel, reverse=reverse, nt=nt, ts=ts)
    return pl.pallas_call(
        kern,
        grid=(b, nt),
        in_specs=[pl.BlockSpec((None, ts, r), lambda i, j: (i, jj(j), 0)),
                  pl.BlockSpec((None, SUBLANES, r),
                               lambda i, j: (i, jnp.maximum(jj(j) * per - 1, 0), 0)),
                  pl.BlockSpec((None, SUBLANES, r),
                               lambda i, j: (i, jnp.minimum((jj(j) + 1) * per, nb8 - 1), 0)),
                  _const_spec((CONV_WIDTH, r)),
                  _const_spec((1, r)),
                  _const_spec((LRU_HEADS, LRU_HEAD_DIM, LRU_HEAD_DIM)),
                  _const_spec((1, r)),
                  _const_spec((LRU_HEADS, LRU_HEAD_DIM, LRU_HEAD_DIM)),
                  _const_spec((1, r)),
                  _const_spec((1, r)),
                  pl.BlockSpec((None, 1, r), lambda i, j: (i, 0, 0))],
        out_specs=pl.BlockSpec((None, ts, r), lambda i, j: (i, jj(j), 0)),
        out_shape=jax.ShapeDtypeStruct((b, s, r), F32),
        scratch_shapes=[pltpu.VMEM((ts, r), F32), pltpu.VMEM((ts, r), F32),
                        pltpu.VMEM((ts, r), F32), pltpu.VMEM((1, r), F32)],
        compiler_params=_cparams(("arbitrary", "arbitrary")),
        name="scan_bwd" if reverse else "scan_fwd",
    )(zx, zx, zx, conv_w, conv_b, wa, ba, wx, bx, lam, h0)


def _proj_in_kernel(x_ref, pos_ref, sh_ref, sc_ref, g_ref, w_ref, wc_ref,
                    ab_ref, zx_ref, gy_ref, sga_ref, sgb_ref, *, f, r, d):
    xp = x_ref[...] + pos_ref[...]
    h = _norm_mod(xp, g_ref[...], sh_ref[...], sc_ref[...]).astype(BF16)
    zf = jnp.dot(h, w_ref[:, 0:f], preferred_element_type=F32)
    ab_ref[...] = jnp.dot(zf.astype(BF16), wc_ref[...], preferred_element_type=F32).astype(BF16)
    zx_ref[...] = jnp.dot(h, w_ref[:, f:f + r], preferred_element_type=F32)
    zy = jnp.dot(h, w_ref[:, f + r:f + 2 * r], preferred_element_type=F32)
    gy_ref[...] = _gelu(zy).astype(BF16)
    zga = jnp.dot(h, w_ref[:, f + 2 * r:f + 2 * r + d], preferred_element_type=F32)
    sga_ref[...] = jax.nn.sigmoid(zga).astype(BF16)
    zgb = jnp.dot(h, w_ref[:, f + 2 * r + d:f + 2 * r + 2 * d], preferred_element_type=F32)
    sgb_ref[...] = jax.nn.sigmoid(zgb).astype(BF16)


def _proj_in(x, pos, mod3, g1, w_in, wc, f, r):
    b, s, d = x.shape
    tt = min(512, s)
    nt = s // tt
    kern = functools.partial(_proj_in_kernel, f=f, r=r, d=d)
    blk = lambda w: pl.BlockSpec((None, tt, w), lambda j, i: (i, j, 0))
    return pl.pallas_call(
        kern,
        grid=(nt, b),
        in_specs=[blk(d),
                  pl.BlockSpec((tt, d), lambda j, i: (j, 0)),
                  pl.BlockSpec((None, 1, d), lambda j, i: (i, 0, 0)),
                  pl.BlockSpec((None, 1, d), lambda j, i: (i, 0, 1)),
                  pl.BlockSpec((1, d), lambda j, i: (0, 0)),
                  _const_spec(w_in.shape),
                  _const_spec(wc.shape)],
        out_specs=[blk(2 * f), blk(r), blk(r), blk(d), blk(d)],
        out_shape=[jax.ShapeDtypeStruct((b, s, 2 * f), BF16),
                   jax.ShapeDtypeStruct((b, s, r), F32),
                   jax.ShapeDtypeStruct((b, s, r), BF16),
                   jax.ShapeDtypeStruct((b, s, d), BF16),
                   jax.ShapeDtypeStruct((b, s, d), BF16)],
        compiler_params=_cparams(("parallel", "parallel")),
        name="proj_in",
    )(x, pos, mod3, mod3, g1, w_in, wc)


def _mix_kernel(x_ref, pos_ref, g1m_ref, sh2_ref, sc2_ref, g2n_ref, hf_ref, hb_ref, gy_ref,
                sga_ref, sgb_ref, ab_ref, ct_ref, st_ref, wf_ref, wl_ref, wo_ref, bo_ref,
                x1_ref, h2t_ref, *, f, tm):
    rows = pl.ds(pl.multiple_of(pl.program_id(1) * tm, tm), tm)
    yf = jnp.dot(ct_ref[rows, :], ab_ref[:, 0:f], preferred_element_type=F32)
    yf = yf - jnp.dot(st_ref[rows, :], ab_ref[:, f:2 * f], preferred_element_type=F32)
    ya = jnp.dot(yf.astype(BF16), wf_ref[...], preferred_element_type=F32)
    hs = (hf_ref[...] + hb_ref[...]) * gy_ref[...].astype(F32)
    yb = jnp.dot(hs.astype(BF16), wl_ref[...], preferred_element_type=F32)
    merged = sga_ref[...].astype(F32) * ya + sgb_ref[...].astype(F32) * yb
    mix = jnp.dot(merged.astype(BF16), wo_ref[...], preferred_element_type=F32) + bo_ref[...]
    x1 = (x_ref[...] + pos_ref[...]) + g1m_ref[...] * mix
    x1_ref[...] = x1
    h2 = _norm_mod(x1, g2n_ref[...], sh2_ref[...], sc2_ref[...])
    h2t_ref[...] = h2.T.astype(BF16)


def _mix(x, pos, mod3, g2n, hf, hb, gy, sga, sgb, ab, ct, st, wf, wl, wo, bo, f):
    b, s, d = x.shape
    r = hf.shape[-1]
    tm = min(256, s)
    nt = s // tm
    kern = functools.partial(_mix_kernel, f=f, tm=tm)
    blk = lambda w: pl.BlockSpec((None, tm, w), lambda i, j: (i, j, 0))
    modrow = lambda k: pl.BlockSpec((None, 1, d), lambda i, j: (i, 0, k))
    return pl.pallas_call(
        kern,
        grid=(b, nt),
        in_specs=[blk(d),
                  pl.BlockSpec((tm, d), lambda i, j: (j, 0)),
                  modrow(2), modrow(3), modrow(4),
                  pl.BlockSpec((1, d), lambda i, j: (0, 0)),
                  blk(r), blk(r), blk(r), blk(d), blk(d),
                  pl.BlockSpec((None, s, 2 * f), lambda i, j: (i, 0, 0),
                               pipeline_mode=pl.Buffered(1)),
                  _const_spec(ct.shape), _const_spec(st.shape),
                  _const_spec(wf.shape), _const_spec(wl.shape), _const_spec(wo.shape),
                  _const_spec(bo.shape)],
        out_specs=[blk(d),
                   pl.BlockSpec((d, tm), lambda i, j: (0, i * nt + j))],
        out_shape=[jax.ShapeDtypeStruct((b, s, d), F32),
                   jax.ShapeDtypeStruct((d, b * s), BF16)],
        compiler_params=_cparams(("parallel", "arbitrary")),
        name="mix",
    )(x, pos, mod3, mod3, mod3, g2n, hf, hb, gy, sga, sgb, ab, ct, st, wf, wl, wo, bo)


def _batcher_pairs(n):
    pairs = []

    def merge(lo, m, r):
        step = r * 2
        if step < m:
            merge(lo, m, step)
            merge(lo + r, m, step)
            pairs.extend((i, i + r) for i in range(lo + r, lo + m - r, step))
        else:
            pairs.append((lo, lo + r))

    def sort(lo, m):
        if m > 1:
            sort(lo, m // 2)
            sort(lo + m // 2, m // 2)
            merge(lo, m, 1)

    sort(0, n)
    return pairs


_SORT16 = _batcher_pairs(PEER_TOPK)


def _exchange(a, i, j):
    hi, lo = jnp.maximum(a[i], a[j]), jnp.minimum(a[i], a[j])
    a[i], a[j] = hi, lo


def _sorted_top16(tiles):
    a = list(tiles)
    for i, j in _SORT16:
        _exchange(a, i, j)
    for shift in (4, 2, 1):
        b = [pltpu.roll(x, shift, axis=0) for x in a]
        a = [jnp.maximum(a[v], b[PEER_TOPK - 1 - v]) for v in range(PEER_TOPK)]
        for stride in (8, 4, 2, 1):
            for i in range(PEER_TOPK):
                if not i & stride:
                    _exchange(a, i, i + stride)
    return a


def _top16(s, kiota):
    rank = jnp.full(s.shape, float(PEER_TOPK), F32)
    for r in range(PEER_TOPK):
        m = jnp.max(s, axis=0, keepdims=True)
        first = jnp.min(jnp.where(s == m, kiota, float(N_KEYS)), axis=0, keepdims=True)
        hit = kiota == first
        s = jnp.where(hit, -jnp.inf, s)
        rank = jnp.where(hit, float(r), rank)
    return rank


def _keyproj_kernel(keys_ref, wq_ref, o_ref):
    o_ref[...] = lax.dot_general(keys_ref[...], wq_ref[...], (((1,), (1,)), ((), ())),
                                 preferred_element_type=F32,
                                 precision=lax.Precision.HIGHEST).astype(BF16)


def _keyproj(keys, wq):
    nhp, nk, hd = keys.shape
    d = wq.shape[0]
    return pl.pallas_call(
        _keyproj_kernel,
        grid=(nhp,),
        in_specs=[pl.BlockSpec((None, nk, hd), lambda i: (i, 0, 0)),
                  pl.BlockSpec((d, hd), lambda i: (0, i))],
        out_specs=pl.BlockSpec((nk, d), lambda i: (i, 0)),
        out_shape=jax.ShapeDtypeStruct((nhp * nk, d), BF16),
        compiler_params=_cparams(("parallel",)),
        name="keyproj",
    )(keys, wq)


def _route_kernel(h2t_ref, kw_ref, rank2_ref, e2_ref, cnt_ref, coef_ref,
                  s_scr, sv_scr, c_scr, z_scr, flag_scr, *, tr):
    nh = PEER_HEADS
    ntile = N_KEYS // SUBLANES
    nk16 = N_KEYS // BF16_ROWS
    s_all = jnp.dot(kw_ref[...], h2t_ref[...], preferred_element_type=F32)
    for hp in range(2 * nh):
        s_scr[hp] = s_all[hp * N_KEYS:(hp + 1) * N_KEYS, :]

    kiota = lax.broadcasted_iota(jnp.int32, (N_KEYS, LANES), 0).astype(F32)

    def lane_block(lb, _):
        l0 = pl.multiple_of(lb * LANES, LANES)
        lanes = pl.ds(l0, LANES)

        def sort_head(h, _):
            hrow = pl.ds(h, 1)
            for p in range(2):
                tiles = [s_scr[2 * h + p, v * SUBLANES:(v + 1) * SUBLANES, lanes]
                         for v in range(ntile)]
                sv = _sorted_top16(tiles)
                for r in range(PEER_TOPK):
                    sv_scr[p, r, hrow, :] = sv[r][0:1, :]
                nsel = jnp.zeros((SUBLANES, LANES), F32)
                for v in range(ntile):
                    nsel = nsel + jnp.where(tiles[v] >= sv[PEER_TOPK - 1], 1.0, 0.0)
                nsel = jnp.sum(nsel, axis=0, keepdims=True)
                flag_scr[p, hrow, :] = jnp.where(nsel != float(PEER_TOPK), 1.0, 0.0)
                if p == 1:
                    ranks = []
                    for v in range(ntile):
                        rank = jnp.zeros((SUBLANES, LANES), F32)
                        for r in range(PEER_TOPK):
                            rank = jnp.where(sv[r] > tiles[v], float(r + 1), rank)
                        ranks.append(rank)
                    for k in range(nk16):
                        pair = jnp.concatenate([ranks[2 * k], ranks[2 * k + 1]], axis=0)
                        rank2_ref[h, k, :, lanes] = pair.astype(BF16)
            return 0

        lax.fori_loop(0, nh, sort_head, 0)

        sv1 = [sv_scr[0, r] for r in range(PEER_TOPK)]
        sv2 = [sv_scr[1, r] for r in range(PEER_TOPK)]
        val = {c: sv1[c[0]] + sv2[c[1]] for c in _CELLS}
        before = {c: 0.0 for c in _CELLS}
        for ia, ca in enumerate(_CELLS):
            for cb in _CELLS[ia + 1:]:
                if ca[0] <= cb[0] and ca[1] <= cb[1]:
                    before[cb] = before[cb] + 1.0
                elif cb[0] <= ca[0] and cb[1] <= ca[1]:
                    before[ca] = before[ca] + 1.0
                else:
                    ge = jnp.where(val[ca] >= val[cb], 1.0, 0.0)
                    before[cb] = before[cb] + ge
                    before[ca] = before[ca] + (1.0 - ge)
        ex1 = [jnp.exp(sv1[r] - sv1[0]) for r in range(PEER_TOPK)]
        ex2 = [jnp.exp(sv2[r] - sv2[0]) for r in range(PEER_TOPK)]
        zsum = jnp.zeros((nh, LANES), F32)
        cnt = [jnp.zeros((nh, LANES), F32) for _ in range(PEER_TOPK)]
        for c in _CELLS:
            sel = before[c] < float(PEER_TOPK)
            cnt[c[0]] = cnt[c[0]] + jnp.where(sel, 1.0, 0.0)
            zsum = zsum + jnp.where(sel, ex1[c[0]] * ex2[c[1]], 0.0)
        for r in range(PEER_TOPK):
            c_scr[r] = cnt[r]
        z_scr[0] = 0.5 / zsum
        z_scr[1] = sv1[0]
        z_scr[2] = sv2[0]
        tie = jnp.maximum(flag_scr[0], flag_scr[1])
        for r in range(PEER_TOPK - 1):
            tie = jnp.where(sv1[r] == sv1[r + 1], 1.0, tie)
            tie = jnp.where(sv2[r] == sv2[r + 1], 1.0, tie)
        flag_scr[2] = tie

        def emit_head(h, _):
            hrow = pl.ds(h, 1)
            tile = lambda ref, *idx: jnp.broadcast_to(ref[(*idx, hrow, slice(None))],
                                                      (SUBLANES, LANES))
            sv1b = [tile(sv_scr, 0, r) for r in range(PEER_TOPK)]
            cb = [tile(c_scr, r) for r in range(PEER_TOPK)]
            gate, max1, max2 = tile(z_scr, 0), tile(z_scr, 1), tile(z_scr, 2)
            e2 = []
            for v in range(ntile):
                rows = slice(v * SUBLANES, (v + 1) * SUBLANES)
                s1 = s_scr[2 * h, rows, lanes]
                cnt1 = jnp.zeros((SUBLANES, LANES), F32)
                for r in range(PEER_TOPK):
                    cnt1 = jnp.where(s1 == sv1b[r], cb[r], cnt1)
                cnt_ref[h, rows, lanes] = cnt1
                coef_ref[h, rows, lanes] = jnp.exp(s1 - max1) * gate
                e2.append(jnp.exp(s_scr[2 * h + 1, rows, lanes] - max2))
            for k in range(nk16):
                pair = jnp.concatenate([e2[2 * k], e2[2 * k + 1]], axis=0)
                e2_ref[h, k, :, lanes] = pair.astype(BF16)
            return 0

        lax.fori_loop(0, nh, emit_head, 0)

        @pl.when(jnp.max(tie) > 0.0)
        def _():
            def fix_head(h, _):
                hrow = pl.ds(h, 1)

                @pl.when(jnp.max(flag_scr[2, hrow, :]) > 0.0)
                def _():
                    rank1 = _top16(s_scr[2 * h, :, lanes], kiota)
                    cnt1 = jnp.zeros((N_KEYS, LANES), F32)
                    for r in range(PEER_TOPK):
                        cnt1 = jnp.where(rank1 == float(r), c_scr[r, hrow, :], cnt1)
                    cnt_ref[h, :, lanes] = cnt1
                    rank2 = _top16(s_scr[2 * h + 1, :, lanes], kiota)
                    for k in range(nk16):
                        rows = slice(k * BF16_ROWS, (k + 1) * BF16_ROWS)
                        rank2_ref[h, k, :, lanes] = rank2[rows].astype(BF16)

                return 0

            lax.fori_loop(0, nh, fix_head, 0)

        return 0

    lax.fori_loop(0, tr // LANES, lane_block, 0)


def _route(h2t, kw):
    d, t = h2t.shape
    nh = PEER_HEADS
    tr = 512
    nk16 = N_KEYS // BF16_ROWS
    kern = functools.partial(_route_kernel, tr=tr)
    bblk = pl.BlockSpec((nh, nk16, BF16_ROWS, tr), lambda i: (0, 0, 0, i))
    fblk = pl.BlockSpec((nh, N_KEYS, tr), lambda i: (0, 0, i))
    return pl.pallas_call(
        kern,
        grid=(t // tr,),
        in_specs=[pl.BlockSpec((d, tr), lambda i: (0, i)),
                  _const_spec(kw.shape)],
        out_specs=[bblk, bblk, fblk, fblk],
        out_shape=[jax.ShapeDtypeStruct((nh, nk16, BF16_ROWS, t), BF16),
                   jax.ShapeDtypeStruct((nh, nk16, BF16_ROWS, t), BF16),
                   jax.ShapeDtypeStruct((nh, N_KEYS, t), F32),
                   jax.ShapeDtypeStruct((nh, N_KEYS, t), F32)],
        scratch_shapes=[pltpu.VMEM((2 * nh, N_KEYS, tr), F32),
                        pltpu.VMEM((2, PEER_TOPK, nh, LANES), F32),
                        pltpu.VMEM((PEER_TOPK, nh, LANES), F32),
                        pltpu.VMEM((3, nh, LANES), F32),
                        pltpu.VMEM((3, nh, LANES), F32)],
        compiler_params=_cparams(("parallel",)),
        name="route",
    )(h2t, kw)


def _peer_kernel(h2t_ref, u_ref, vt_ref, rank2_ref, e2_ref, cnt_ref, coef_ref, x1_ref,
                 g2m_ref, gf_ref, o_ref, acc_ref, wt_ref, *, eb, ne, tm):
    e = pl.program_id(1)
    nk16 = N_KEYS // BF16_ROWS

    @pl.when(e == 0)
    def _():
        acc_ref[...] = jnp.zeros_like(acc_ref)

    st = jnp.dot(u_ref[...], h2t_ref[...], preferred_element_type=F32)
    for k in range(eb // N_KEYS):
        i1 = e * (eb // N_KEYS) + k
        cnt = [jnp.broadcast_to(cnt_ref[h, pl.ds(i1, 1), :], (BF16_ROWS, tm)).astype(BF16)
               for h in range(PEER_HEADS)]
        cf = [jnp.broadcast_to(coef_ref[h, pl.ds(i1, 1), :], (BF16_ROWS, tm)).astype(BF16)
              for h in range(PEER_HEADS)]
        for c in range(nk16):
            m = None
            for h in range(PEER_HEADS):
                term = jnp.where(rank2_ref[h, c] < cnt[h], e2_ref[h, c] * cf[h],
                                 jnp.zeros((), BF16))
                m = term if m is None else m + term
            r0 = k * N_KEYS + c * BF16_ROWS
            act2 = _gelu_x2(st[r0:r0 + BF16_ROWS, :].astype(BF16))
            wt_ref[r0:r0 + BF16_ROWS, :] = m * act2
    acc_ref[...] += jnp.dot(vt_ref[...], wt_ref[...], preferred_element_type=F32)

    @pl.when(e == ne - 1)
    def _():
        x2 = x1_ref[...] + g2m_ref[...] * acc_ref[...].T
        ms = jnp.mean(x2 * x2, axis=-1, keepdims=True)
        o_ref[...] = (x2 * lax.rsqrt(ms + NORM_EPS)) * gf_ref[...]


def _peer(h2t, u, vt, rank2, e2, cnt, coef, x1, mod3, gf):
    b, s, d = x1.shape
    t = b * s
    ne_total = u.shape[0]
    tm = min(512, s)
    eb = 2048
    ne = ne_total // eb
    per_b = s // tm
    nk16 = N_KEYS // BF16_ROWS
    kern = functools.partial(_peer_kernel, eb=eb, ne=ne, tm=tm)
    bblk = pl.BlockSpec((PEER_HEADS, nk16, BF16_ROWS, tm), lambda i, e: (0, 0, 0, i))
    fblk = pl.BlockSpec((PEER_HEADS, N_KEYS, tm), lambda i, e: (0, 0, i))
    xblk = pl.BlockSpec((None, tm, d), lambda i, e: (i // per_b, i % per_b, 0))
    return pl.pallas_call(
        kern,
        grid=(t // tm, ne),
        in_specs=[pl.BlockSpec((d, tm), lambda i, e: (0, i)),
                  pl.BlockSpec((eb, d), lambda i, e: (e, 0)),
                  pl.BlockSpec((d, eb), lambda i, e: (0, e)),
                  bblk, bblk, fblk, fblk,
                  xblk,
                  pl.BlockSpec((None, 1, d), lambda i, e: (i // per_b, 0, 5)),
                  pl.BlockSpec((1, d), lambda i, e: (0, 0))],
        out_specs=xblk,
        out_shape=jax.ShapeDtypeStruct((b, s, d), F32),
        scratch_shapes=[pltpu.VMEM((d, tm), F32), pltpu.VMEM((eb, tm), BF16)],
        compiler_params=_cparams(("parallel", "arbitrary")),
        name="peer",
    )(h2t, u, vt, rank2, e2, cnt, coef, x1, mod3, gf)


def _sincos_2d(num_tokens, dim):
    rows = num_tokens // GRID_W
    row = jnp.repeat(jnp.arange(rows, dtype=F32), GRID_W)
    col = jnp.tile(jnp.arange(GRID_W, dtype=F32), rows)
    nf = dim // 4
    omega = 1.0 / (POS_BASE ** (jnp.arange(nf, dtype=F32) / nf))
    er = row[:, None] * omega[None]
    ec = col[:, None] * omega[None]
    return jnp.concatenate([jnp.sin(er), jnp.cos(er), jnp.sin(ec), jnp.cos(ec)], axis=-1)


def _dft_tables(n):
    q = 64 if n % 64 == 0 and n > 64 else n
    p = n // q
    k = jnp.arange(n, dtype=jnp.int32)[:, None]
    ang1 = ((k * jnp.arange(p, dtype=jnp.int32)[None, :]) % p).astype(F32) * (2.0 * math.pi / p)
    ang2 = ((k * jnp.arange(q, dtype=jnp.int32)[None, :]) % n).astype(F32) * (2.0 * math.pi / n)
    c1, s1 = jnp.cos(ang1)[:, :, None], jnp.sin(ang1)[:, :, None]
    c2, s2 = jnp.cos(ang2)[:, None, :], jnp.sin(ang2)[:, None, :]
    return (c1 * c2 - s1 * s2).reshape(n, n), (s1 * c2 + c1 * s2).reshape(n, n)


def kernel(x, c, ctx, c_ctx, w_mod, b_mod, norm1_g, norm2_g, w_in, w_fourier, conv_w, conv_b,
           lru_wa, lru_ba, lru_wx, lru_bx, lru_lambda, w_lru_out, w_out, b_out,
           peer_wq, peer_keys, peer_u, peer_v, final_g):
    b, s, d = x.shape
    ctx_len = ctx.shape[1]
    f = FOURIER_GROUPS * FOURIER_GROUP_DIM
    r = LRU_HEADS * LRU_HEAD_DIM
    assert w_mod.shape[0] == 1, "single-layer block"
    assert w_in.shape[2] == f + 2 * r + 2 * d

    pos = _sincos_2d(s, d)
    ct, st = _dft_tables(s)
    cc, sc = _dft_tables(FOURIER_GROUP_DIM)
    scale = 1.0 / math.sqrt(s * FOURIER_GROUP_DIM)
    eye = jnp.eye(FOURIER_GROUPS, dtype=F32)
    wc = jnp.concatenate([jnp.kron(eye, cc), jnp.kron(eye, sc)], axis=1) * scale

    rp = -(-(b + 1) // SUBLANES) * SUBLANES
    cond = jnp.concatenate([c, c_ctx[None], jnp.zeros((rp - b - 1, d), F32)], axis=0)
    mod = _adaln(cond, w_mod[0], b_mod)
    mod3 = mod.reshape(rp, 1, 6 * d)

    w_in_b = w_in[0].astype(BF16)
    g1 = norm1_g
    g2 = norm2_g
    wa = (0.5 * lru_wa[0]).astype(BF16)
    wx = (0.5 * lru_wx[0]).astype(BF16)
    ba, bx, lam = 0.5 * lru_ba[0], 0.5 * lru_bx[0], lru_lambda[0]

    def scans(zx, h0f, h0b):
        hf = _scan(zx, conv_w[0], conv_b, wa[0], ba[0:1], wx[0], bx[0:1], lam[0:1], h0f, False)
        hb = _scan(zx, conv_w[0], conv_b, wa[1], ba[1:2], wx[1], bx[1:2], lam[1:2], h0b, True)
        return hf, hb

    zx_c = _ctx_proj(ctx.reshape(b * ctx_len, d), mod3, b, g1, w_in_b[:, f:f + r])
    zeros = jnp.zeros((b, 1, r), F32)
    hf_c, hb_c = scans(zx_c.reshape(b, ctx_len, r), zeros, zeros)
    htf = hf_c[:, ctx_len - 1:ctx_len, :]
    htb = hb_c[:, 0:1, :]

    ab, zx, gy, sga, sgb = _proj_in(x, pos, mod3, g1, w_in_b, wc.astype(BF16), f, r)
    hf, hb = scans(zx, htf, htb)
    x1, h2t = _mix(x, pos, mod3, g2, hf, hb, gy, sga, sgb, ab,
                   ct.astype(BF16), st.astype(BF16),
                   w_fourier[0].astype(BF16), w_lru_out[0].astype(BF16),
                   w_out[0].astype(BF16), b_out, f)

    keys = peer_keys[0].reshape(2 * PEER_HEADS, N_KEYS, PEER_HALF_DIM)
    rank2, e2, cnt, coef = _route(h2t, _keyproj(keys, peer_wq[0]))
    u = peer_u[0].astype(BF16)
    vt = peer_v[0].T.astype(BF16)
    return _peer(h2t, u, vt, rank2, e2, cnt, coef, x1, mod3, final_g[None])
```

```python
import functools
import math

import jax
import jax.numpy as jnp
from jax import lax
from jax.experimental import pallas as pl
from jax.experimental.pallas import tpu as pltpu

F32 = jnp.float32
BF16 = jnp.bfloat16

NORM_EPS = 1e-6
POS_BASE = 10000.0
GRID_W = 64
FOURIER_GROUPS = 4
FOURIER_GROUP_DIM = 128
LRU_HEADS = 8
LRU_HEAD_DIM = 128
CONV_WIDTH = 4
LRU_C = 8.0
PEER_HEADS = 8
PEER_HALF_DIM = 128
N_KEYS = 128
PEER_TOPK = 16

LANES = 128
SUBLANES = 8
BF16_ROWS = 16
VMEM_LIMIT = 56 * 1024 * 1024

_CELLS = [(r1, r2) for r1 in range(PEER_TOPK) for r2 in range(PEER_TOPK)
          if (r1 + 1) * (r2 + 1) <= PEER_TOPK]


def _cparams(sem):
    return pltpu.CompilerParams(dimension_semantics=sem, vmem_limit_bytes=VMEM_LIMIT)


def _const_spec(shape):
    nd = len(shape)
    return pl.BlockSpec(shape, lambda *_: (0,) * nd, pipeline_mode=pl.Buffered(1))


def _gelu(x):
    c = math.sqrt(2.0 / math.pi)
    return x * (0.5 * (1.0 + jnp.tanh(c * (x + 0.044715 * (x * x * x)))))


def _gelu_x2(x):
    c = math.sqrt(2.0 / math.pi)
    return x * (1.0 + jnp.tanh(x * (c + (c * 0.044715) * (x * x))))


def _norm_mod(x, g, sh, sc):
    ms = jnp.mean(x * x, axis=-1, keepdims=True)
    y = x * lax.rsqrt(ms + NORM_EPS)
    return (y * g) * (1.0 + sc) + sh


def _adaln_kernel(c_ref, w_ref, b_ref, o_ref):
    c = c_ref[...]
    s = c * jax.nn.sigmoid(c)
    o_ref[...] = jnp.dot(s, w_ref[...], preferred_element_type=F32,
                         precision=lax.Precision.HIGHEST) + b_ref[...]


def _adaln(cc, w_mod, b_mod):
    rp, d = cc.shape
    n = w_mod.shape[1]
    tn = 1536
    return pl.pallas_call(
        _adaln_kernel,
        grid=(n // tn,),
        in_specs=[pl.BlockSpec((rp, d), lambda j: (0, 0)),
                  pl.BlockSpec((d, tn), lambda j: (0, j)),
                  pl.BlockSpec((1, tn), lambda j: (0, j))],
        out_specs=pl.BlockSpec((rp, tn), lambda j: (0, j)),
        out_shape=jax.ShapeDtypeStruct((rp, n), F32),
        compiler_params=_cparams(("parallel",)),
        name="adaln",
    )(cc, w_mod, b_mod)


def _ctx_proj_kernel(x_ref, sh_ref, sc_ref, g_ref, w_ref, o_ref):
    h = _norm_mod(x_ref[...], g_ref[...], sh_ref[...], sc_ref[...])
    o_ref[...] = jnp.dot(h.astype(BF16), w_ref[...], preferred_element_type=F32)


def _ctx_proj(ctx2, mod3, row, g1, w_zx):
    n, d = ctx2.shape
    r = w_zx.shape[1]
    tt = min(512, n)
    return pl.pallas_call(
        _ctx_proj_kernel,
        grid=(n // tt,),
        in_specs=[pl.BlockSpec((tt, d), lambda i: (i, 0)),
                  pl.BlockSpec((None, 1, d), lambda i: (row, 0, 0)),
                  pl.BlockSpec((None, 1, d), lambda i: (row, 0, 1)),
                  pl.BlockSpec((1, d), lambda i: (0, 0)),
                  _const_spec((d, r))],
        out_specs=pl.BlockSpec((tt, r), lambda i: (i, 0)),
        out_shape=jax.ShapeDtypeStruct((n, r), F32),
        compiler_params=_cparams(("parallel",)),
        name="ctx_proj",
    )(ctx2, mod3, mod3, g1, w_zx)


def _scan_kernel(zc_ref, zp_ref, zn_ref, cw_ref, cb_ref, wa_ref, ba_ref, wx_ref, bx_ref,
                 lam_ref, h0_ref, o_ref, a_scr, b_scr, xc_scr, h_scr, *, reverse, nt, ts):
    j = pl.program_id(1)
    jj = (nt - 1 - j) if reverse else j

    @pl.when(j == 0)
    def _():
        h_scr[...] = h0_ref[...]

    cur = zc_ref[...]
    has_prev = (jj > 0).astype(F32)
    has_next = (jj < nt - 1).astype(F32)
    prev_row = zp_ref[SUBLANES - 1:SUBLANES, :] * has_prev
    next0 = zn_ref[0:1, :] * has_next
    next1 = zn_ref[1:2, :] * has_next
    w = [cw_ref[k:k + 1, :] for k in range(CONV_WIDTH)]

    def conv(zm1, z0, zp1, zp2):
        return (((cb_ref[...] + zm1 * w[0]) + z0 * w[1]) + zp1 * w[2]) + zp2 * w[3]

    xc_scr[...] = conv(pltpu.roll(cur, 1, axis=0), cur, pltpu.roll(cur, ts - 1, axis=0),
                       pltpu.roll(cur, ts - 2, axis=0))
    z = lambda i: zc_ref[i:i + 1, :]
    xc_scr[0:1, :] = conv(prev_row, z(0), z(1), z(2))
    xc_scr[ts - 2:ts - 1, :] = conv(z(ts - 3), z(ts - 2), z(ts - 1), next0)
    xc_scr[ts - 1:ts, :] = conv(z(ts - 2), z(ts - 1), next0, next1)

    lam = lam_ref[...]
    softplus_neg = jnp.maximum(-lam, 0.0) + jnp.log1p(jnp.exp(-jnp.abs(lam)))
    half_c = (0.5 * LRU_C) * softplus_neg
    for hd in range(LRU_HEADS):
        cols = slice(hd * LRU_HEAD_DIM, (hd + 1) * LRU_HEAD_DIM)
        xh = xc_scr[:, cols]
        xb = xh.astype(BF16)
        hzr = jnp.dot(xb, wa_ref[hd], preferred_element_type=F32) + ba_ref[:, cols]
        hzi = jnp.dot(xb, wx_ref[hd], preferred_element_type=F32) + bx_ref[:, cols]
        neg_log_a = half_c[:, cols] * jnp.tanh(hzr) + half_c[:, cols]
        a = jnp.exp(-neg_log_a)
        a_scr[:, cols] = a
        y = jnp.tanh(neg_log_a) * (a * a + 1.0)
        root = jnp.where(y > 0.0, y * lax.rsqrt(y), 0.0)
        b_scr[:, cols] = root * ((0.5 * jnp.tanh(hzi) + 0.5) * xh)

    def step(k, h):
        t = (ts - 1 - k) if reverse else k
        h = a_scr[pl.ds(t, 1), :] * h + b_scr[pl.ds(t, 1), :]
        o_ref[pl.ds(t, 1), :] = h
        return h

    h_scr[...] = lax.fori_loop(0, ts, step, h_scr[...], unroll=8)


def _scan(zx, conv_w, conv_b, wa, ba, wx, bx, lam, h0, reverse):
    b, s, r = zx.shape
    ts = min(512, s)
    nt = s // ts
    nb8 = s // SUBLANES
    per = ts // SUBLANES

    def jj(j):
        return (nt - 1 - j) if reverse else j

    kern = functools.partial(_scan_kernel, reverse=reverse, nt=nt, ts=ts)
    return pl.pallas_call(
        kern,
        grid=(b, nt),
        in_specs=[pl.BlockSpec((None, ts, r), lambda i, j: (i, jj(j), 0)),
                  pl.BlockSpec((None, SUBLANES, r),
                               lambda i, j: (i, jnp.maximum(jj(j) * per - 1, 0), 0)),
                  pl.BlockSpec((None, SUBLANES, r),
                               lambda i, j: (i, jnp.minimum((jj(j) + 1) * per, nb8 - 1), 0)),
                  _const_spec((CONV_WIDTH, r)),
                  _const_spec((1, r)),
                  _const_spec((LRU_HEADS, LRU_HEAD_DIM, LRU_HEAD_DIM)),
                  _const_spec((1, r)),
                  _const_spec((LRU_HEADS, LRU_HEAD_DIM, LRU_HEAD_DIM)),
                  _const_spec((1, r)),
                  _const_spec((1, r)),
                  pl.BlockSpec((None, 1, r), lambda i, j: (i, 0, 0))],
        out_specs=pl.BlockSpec((None, ts, r), lambda i, j: (i, jj(j), 0)),
        out_shape=jax.ShapeDtypeStruct((b, s, r), F32),
        scratch_shapes=[pltpu.VMEM((ts, r), F32), pltpu.VMEM((ts, r), F32),
                        pltpu.VMEM((ts, r), F32), pltpu.VMEM((1, r), F32)],
        compiler_params=_cparams(("arbitrary", "arbitrary")),
        name="scan_bwd" if reverse else "scan_fwd",
    )(zx, zx, zx, conv_w, conv_b, wa, ba, wx, bx, lam, h0)


def _proj_in_kernel(x_ref, pos_ref, sh_ref, sc_ref, g_ref, w_ref, wc_ref,
                    ab_ref, zx_ref, gy_ref, sga_ref, sgb_ref, *, f, r, d):
    xp = x_ref[...] + pos_ref[...]
    h = _norm_mod(xp, g_ref[...], sh_ref[...], sc_ref[...]).astype(BF16)
    zf = jnp.dot(h, w_ref[:, 0:f], preferred_element_type=F32)
    ab_ref[...] = jnp.dot(zf.astype(BF16), wc_ref[...], preferred_element_type=F32).astype(BF16)
    zx_ref[...] = jnp.dot(h, w_ref[:, f:f + r], preferred_element_type=F32)
    zy = jnp.dot(h, w_ref[:, f + r:f + 2 * r], preferred_element_type=F32)
    gy_ref[...] = _gelu(zy).astype(BF16)
    zga = jnp.dot(h, w_ref[:, f + 2 * r:f + 2 * r + d], preferred_element_type=F32)
    sga_ref[...] = jax.nn.sigmoid(zga).astype(BF16)
    zgb = jnp.dot(h, w_ref[:, f + 2 * r + d:f + 2 * r + 2 * d], preferred_element_type=F32)
    sgb_ref[...] = jax.nn.sigmoid(zgb).astype(BF16)


def _proj_in(x, pos, mod3, g1, w_in, wc, f, r):
    b, s, d = x.shape
    tt = min(512, s)
    nt = s // tt
    kern = functools.partial(_proj_in_kernel, f=f, r=r, d=d)
    blk = lambda w: pl.BlockSpec((None, tt, w), lambda j, i: (i, j, 0))
    return pl.pallas_call(
        kern,
        grid=(nt, b),
        in_specs=[blk(d),
                  pl.BlockSpec((tt, d), lambda j, i: (j, 0)),
                  pl.BlockSpec((None, 1, d), lambda j, i: (i, 0, 0)),
                  pl.BlockSpec((None, 1, d), lambda j, i: (i, 0, 1)),
                  pl.BlockSpec((1, d), lambda j, i: (0, 0)),
                  _const_spec(w_in.shape),
                  _const_spec(wc.shape)],
        out_specs=[blk(2 * f), blk(r), blk(r), blk(d), blk(d)],
        out_shape=[jax.ShapeDtypeStruct((b, s, 2 * f), BF16),
                   jax.ShapeDtypeStruct((b, s, r), F32),
                   jax.ShapeDtypeStruct((b, s, r), BF16),
                   jax.ShapeDtypeStruct((b, s, d), BF16),
                   jax.ShapeDtypeStruct((b, s, d), BF16)],
        compiler_params=_cparams(("parallel", "parallel")),
        name="proj_in",
    )(x, pos, mod3, mod3, g1, w_in, wc)


def _mix_kernel(x_ref, pos_ref, g1m_ref, sh2_ref, sc2_ref, g2n_ref, hf_ref, hb_ref, gy_ref,
                sga_ref, sgb_ref, ab_ref, ct_ref, st_ref, wf_ref, wl_ref, wo_ref, bo_ref,
                x1_ref, h2t_ref, *, f, tm):
    rows = pl.ds(pl.multiple_of(pl.program_id(1) * tm, tm), tm)
    yf = jnp.dot(ct_ref[rows, :], ab_ref[:, 0:f], preferred_element_type=F32)
    yf = yf - jnp.dot(st_ref[rows, :], ab_ref[:, f:2 * f], preferred_element_type=F32)
    ya = jnp.dot(yf.astype(BF16), wf_ref[...], preferred_element_type=F32)
    hs = (hf_ref[...] + hb_ref[...]) * gy_ref[...].astype(F32)
    yb = jnp.dot(hs.astype(BF16), wl_ref[...], preferred_element_type=F32)
    merged = sga_ref[...].astype(F32) * ya + sgb_ref[...].astype(F32) * yb
    mix = jnp.dot(merged.astype(BF16), wo_ref[...], preferred_element_type=F32) + bo_ref[...]
    x1 = (x_ref[...] + pos_ref[...]) + g1m_ref[...] * mix
    x1_ref[...] = x1
    h2 = _norm_mod(x1, g2n_ref[...], sh2_ref[...], sc2_ref[...])
    h2t_ref[...] = h2.T.astype(BF16)


def _mix(x, pos, mod3, g2n, hf, hb, gy, sga, sgb, ab, ct, st, wf, wl, wo, bo, f):
    b, s, d = x.shape
    r = hf.shape[-1]
    tm = min(256, s)
    nt = s // tm
    kern = functools.partial(_mix_kernel, f=f, tm=tm)
    blk = lambda w: pl.BlockSpec((None, tm, w), lambda i, j: (i, j, 0))
    modrow = lambda k: pl.BlockSpec((None, 1, d), lambda i, j: (i, 0, k))
    return pl.pallas_call(
        kern,
        grid=(b, nt),
        in_specs=[blk(d),
                  pl.BlockSpec((tm, d), lambda i, j: (j, 0)),
                  modrow(2), modrow(3), modrow(4),
                  pl.BlockSpec((1, d), lambda i, j: (0, 0)),
                  blk(r), blk(r), blk(r), blk(d), blk(d),
                  pl.BlockSpec((None, s, 2 * f), lambda i, j: (i, 0, 0),
                               pipeline_mode=pl.Buffered(1)),
                  _const_spec(ct.shape), _const_spec(st.shape),
                  _const_spec(wf.shape), _const_spec(wl.shape), _const_spec(wo.shape),
                  _const_spec(bo.shape)],
        out_specs=[blk(d),
                   pl.BlockSpec((d, tm), lambda i, j: (0, i * nt + j))],
        out_shape=[jax.ShapeDtypeStruct((b, s, d), F32),
                   jax.ShapeDtypeStruct((d, b * s), BF16)],
        compiler_params=_cparams(("parallel", "arbitrary")),
        name="mix",
    )(x, pos, mod3, mod3, mod3, g2n, hf, hb, gy, sga, sgb, ab, ct, st, wf, wl, wo, bo)


def _batcher_pairs(n):
    pairs = []

    def merge(lo, m, r):
        step = r * 2
        if step < m:
            merge(lo, m, step)
            merge(lo + r, m, step)
            pairs.extend((i, i + r) for i in range(lo + r, lo + m - r, step))
        else:
            pairs.append((lo, lo + r))

    def sort(lo, m):
        if m > 1:
            sort(lo, m // 2)
            sort(lo + m // 2, m // 2)
            merge(lo, m, 1)

    sort(0, n)
    return pairs


_SORT16 = _batcher_pairs(PEER_TOPK)


def _exchange(a, i, j):
    hi, lo = jnp.maximum(a[i], a[j]), jnp.minimum(a[i], a[j])
    a[i], a[j] = hi, lo


def _sorted_top16(tiles):
    a = list(tiles)
    for i, j in _SORT16:
        _exchange(a, i, j)
    for shift in (4, 2, 1):
        b = [pltpu.roll(x, shift, axis=0) for x in a]
        a = [jnp.maximum(a[v], b[PEER_TOPK - 1 - v]) for v in range(PEER_TOPK)]
        for stride in (8, 4, 2, 1):
            for i in range(PEER_TOPK):
                if not i & stride:
                    _exchange(a, i, i + stride)
    return a


def _top16(s, kiota):
    rank = jnp.full(s.shape, float(PEER_TOPK), F32)
    for r in range(PEER_TOPK):
        m = jnp.max(s, axis=0, keepdims=True)
        first = jnp.min(jnp.where(s == m, kiota, float(N_KEYS)), axis=0, keepdims=True)
        hit = kiota == first
        s = jnp.where(hit, -jnp.inf, s)
        rank = jnp.where(hit, float(r), rank)
    return rank


def _keyproj_kernel(keys_ref, wq_ref, o_ref):
    o_ref[...] = lax.dot_general(keys_ref[...], wq_ref[...], (((1,), (1,)), ((), ())),
                                 preferred_element_type=F32,
                                 precision=lax.Precision.HIGHEST).astype(BF16)


def _keyproj(keys, wq):
    nhp, nk, hd = keys.shape
    d = wq.shape[0]
    return pl.pallas_call(
        _keyproj_kernel,
        grid=(nhp,),
        in_specs=[pl.BlockSpec((None, nk, hd), lambda i: (i, 0, 0)),
                  pl.BlockSpec((d, hd), lambda i: (0, i))],
        out_specs=pl.BlockSpec((nk, d), lambda i: (i, 0)),
        out_shape=jax.ShapeDtypeStruct((nhp * nk, d), BF16),
        compiler_params=_cparams(("parallel",)),
        name="keyproj",
    )(keys, wq)


def _route_kernel(h2t_ref, kw_ref, rank2_ref, e2_ref, cnt_ref, coef_ref,
                  s_scr, sv_scr, c_scr, z_scr, flag_scr, *, tr):
    nh = PEER_HEADS
    ntile = N_KEYS // SUBLANES
    nk16 = N_KEYS // BF16_ROWS
    s_all = jnp.dot(kw_ref[...], h2t_ref[...], preferred_element_type=F32)
    for hp in range(2 * nh):
        s_scr[hp] = s_all[hp * N_KEYS:(hp + 1) * N_KEYS, :]

    kiota = lax.broadcasted_iota(jnp.int32, (N_KEYS, LANES), 0).astype(F32)

    def lane_block(lb, _):
        l0 = pl.multiple_of(lb * LANES, LANES)
        lanes = pl.ds(l0, LANES)

        def sort_head(h, _):
            hrow = pl.ds(h, 1)
            for p in range(2):
                tiles = [s_scr[2 * h + p, v * SUBLANES:(v + 1) * SUBLANES, lanes]
                         for v in range(ntile)]
                sv = _sorted_top16(tiles)
                for r in range(PEER_TOPK):
                    sv_scr[p, r, hrow, :] = sv[r][0:1, :]
                nsel = jnp.zeros((SUBLANES, LANES), F32)
                for v in range(ntile):
                    nsel = nsel + jnp.where(tiles[v] >= sv[PEER_TOPK - 1], 1.0, 0.0)
                nsel = jnp.sum(nsel, axis=0, keepdims=True)
                flag_scr[p, hrow, :] = jnp.where(nsel != float(PEER_TOPK), 1.0, 0.0)
                if p == 1:
                    ranks = []
                    for v in range(ntile):
                        rank = jnp.zeros((SUBLANES, LANES), F32)
                        for r in range(PEER_TOPK):
                            rank = jnp.where(sv[r] > tiles[v], float(r + 1), rank)
                        ranks.append(rank)
                    for k in range(nk16):
                        pair = jnp.concatenate([ranks[2 * k], ranks[2 * k + 1]], axis=0)
                        rank2_ref[h, k, :, lanes] = pair.astype(BF16)
            return 0

        lax.fori_loop(0, nh, sort_head, 0)

        sv1 = [sv_scr[0, r] for r in range(PEER_TOPK)]
        sv2 = [sv_scr[1, r] for r in range(PEER_TOPK)]
        val = {c: sv1[c[0]] + sv2[c[1]] for c in _CELLS}
        before = {c: 0.0 for c in _CELLS}
        for ia, ca in enumerate(_CELLS):
            for cb in _CELLS[ia + 1:]:
                if ca[0] <= cb[0] and ca[1] <= cb[1]:
                    before[cb] = before[cb] + 1.0
                elif cb[0] <= ca[0] and cb[1] <= ca[1]:
                    before[ca] = before[ca] + 1.0
                else:
                    ge = jnp.where(val[ca] >= val[cb], 1.0, 0.0)
                    before[cb] = before[cb] + ge
                    before[ca] = before[ca] + (1.0 - ge)
        ex1 = [jnp.exp(sv1[r] - sv1[0]) for r in range(PEER_TOPK)]
        ex2 = [jnp.exp(sv2[r] - sv2[0]) for r in range(PEER_TOPK)]
        zsum = jnp.zeros((nh, LANES), F32)
        cnt = [jnp.zeros((nh, LANES), F32) for _ in range(PEER_TOPK)]
        for c in _CELLS:
            sel = before[c] < float(PEER_TOPK)
            cnt[c[0]] = cnt[c[0]] + jnp.where(sel, 1.0, 0.0)
            zsum = zsum + jnp.where(sel, ex1[c[0]] * ex2[c[1]], 0.0)
        for r in range(PEER_TOPK):
            c_scr[r] = cnt[r]
        z_scr[0] = 0.5 / zsum
        z_scr[1] = sv1[0]
        z_scr[2] = sv2[0]
        tie = jnp.maximum(flag_scr[0], flag_scr[1])
        for r in range(PEER_TOPK - 1):
            tie = jnp.where(sv1[r] == sv1[r + 1], 1.0, tie)
            tie = jnp.where(sv2[r] == sv2[r + 1], 1.0, tie)
        flag_scr[2] = tie

        def emit_head(h, _):
            hrow = pl.ds(h, 1)
            tile = lambda ref, *idx: jnp.broadcast_to(ref[(*idx, hrow, slice(None))],
                                                      (SUBLANES, LANES))
            sv1b = [tile(sv_scr, 0, r) for r in range(PEER_TOPK)]
            cb = [tile(c_scr, r) for r in range(PEER_TOPK)]
            gate, max1, max2 = tile(z_scr, 0), tile(z_scr, 1), tile(z_scr, 2)
            e2 = []
            for v in range(ntile):
                rows = slice(v * SUBLANES, (v + 1) * SUBLANES)
                s1 = s_scr[2 * h, rows, lanes]
                cnt1 = jnp.zeros((SUBLANES, LANES), F32)
                for r in range(PEER_TOPK):
                    cnt1 = jnp.where(s1 == sv1b[r], cb[r], cnt1)
                cnt_ref[h, rows, lanes] = cnt1
                coef_ref[h, rows, lanes] = jnp.exp(s1 - max1) * gate
                e2.append(jnp.exp(s_scr[2 * h + 1, rows, lanes] - max2))
            for k in range(nk16):
                pair = jnp.concatenate([e2[2 * k], e2[2 * k + 1]], axis=0)
                e2_ref[h, k, :, lanes] = pair.astype(BF16)
            return 0

        lax.fori_loop(0, nh, emit_head, 0)

        @pl.when(jnp.max(tie) > 0.0)
        def _():
            def fix_head(h, _):
                hrow = pl.ds(h, 1)

                @pl.when(jnp.max(flag_scr[2, hrow, :]) > 0.0)
                def _():
                    rank1 = _top16(s_scr[2 * h, :, lanes], kiota)
                    cnt1 = jnp.zeros((N_KEYS, LANES), F32)
                    for r in range(PEER_TOPK):
                        cnt1 = jnp.where(rank1 == float(r), c_scr[r, hrow, :], cnt1)
                    cnt_ref[h, :, lanes] = cnt1
                    rank2 = _top16(s_scr[2 * h + 1, :, lanes], kiota)
                    for k in range(nk16):
                        rows = slice(k * BF16_ROWS, (k + 1) * BF16_ROWS)
                        rank2_ref[h, k, :, lanes] = rank2[rows].astype(BF16)

                return 0

            lax.fori_loop(0, nh, fix_head, 0)

        return 0

    lax.fori_loop(0, tr // LANES, lane_block, 0)


def _route(h2t, kw):
    d, t = h2t.shape
    nh = PEER_HEADS
    tr = 512
    nk16 = N_KEYS // BF16_ROWS
    kern = functools.partial(_route_kernel, tr=tr)
    bblk = pl.BlockSpec((nh, nk16, BF16_ROWS, tr), lambda i: (0, 0, 0, i))
    fblk = pl.BlockSpec((nh, N_KEYS, tr), lambda i: (0, 0, i))
    return pl.pallas_call(
        kern,
        grid=(t // tr,),
        in_specs=[pl.BlockSpec((d, tr), lambda i: (0, i)),
                  _const_spec(kw.shape)],
        out_specs=[bblk, bblk, fblk, fblk],
        out_shape=[jax.ShapeDtypeStruct((nh, nk16, BF16_ROWS, t), BF16),
                   jax.ShapeDtypeStruct((nh, nk16, BF16_ROWS, t), BF16),
                   jax.ShapeDtypeStruct((nh, N_KEYS, t), F32),
                   jax.ShapeDtypeStruct((nh, N_KEYS, t), F32)],
        scratch_shapes=[pltpu.VMEM((2 * nh, N_KEYS, tr), F32),
                        pltpu.VMEM((2, PEER_TOPK, nh, LANES), F32),
                        pltpu.VMEM((PEER_TOPK, nh, LANES), F32),
                        pltpu.VMEM((3, nh, LANES), F32),
                        pltpu.VMEM((3, nh, LANES), F32)],
        compiler_params=_cparams(("parallel",)),
        name="route",
    )(h2t, kw)


def _peer_kernel(h2t_ref, u_ref, vt_ref, rank2_ref, e2_ref, cnt_ref, coef_ref, x1_ref,
                 g2m_ref, gf_ref, o_ref, acc_ref, wt_ref, st_ref, *, eb, ne, tm):
    e = pl.program_id(1)
    nk16 = N_KEYS // BF16_ROWS

    @pl.when(e == 0)
    def _():
        acc_ref[...] = jnp.zeros_like(acc_ref)

    nchunk = 2
    crow = eb // nchunk
    for j in range(nchunk):
        rows = slice(j * crow, (j + 1) * crow)
        st_ref[rows, :] = jnp.dot(u_ref[rows, :], h2t_ref[...],
                                  preferred_element_type=F32).astype(BF16)
        for k in range(j * crow // N_KEYS, (j + 1) * crow // N_KEYS):
            i1 = e * (eb // N_KEYS) + k
            cnt = [jnp.broadcast_to(cnt_ref[h, pl.ds(i1, 1), :], (BF16_ROWS, tm)).astype(BF16)
                   for h in range(PEER_HEADS)]
            cf = [jnp.broadcast_to(coef_ref[h, pl.ds(i1, 1), :], (BF16_ROWS, tm)).astype(BF16)
                  for h in range(PEER_HEADS)]
            for c in range(nk16):
                m = None
                for h in range(PEER_HEADS):
                    term = jnp.where(rank2_ref[h, c] < cnt[h], e2_ref[h, c] * cf[h],
                                     jnp.zeros((), BF16))
                    m = term if m is None else m + term
                r0 = k * N_KEYS + c * BF16_ROWS
                wt_ref[r0:r0 + BF16_ROWS, :] = m
    for r0 in range(0, eb, 8 * BF16_ROWS):
        rows = slice(r0, r0 + 8 * BF16_ROWS)
        wt_ref[rows, :] = wt_ref[rows, :] * _gelu_x2(st_ref[rows, :])
    acc_ref[...] += jnp.dot(vt_ref[...], wt_ref[...], preferred_element_type=F32)

    @pl.when(e == ne - 1)
    def _():
        x2 = x1_ref[...] + g2m_ref[...] * acc_ref[...].T
        ms = jnp.mean(x2 * x2, axis=-1, keepdims=True)
        o_ref[...] = (x2 * lax.rsqrt(ms + NORM_EPS)) * gf_ref[...]


def _peer(h2t, u, vt, rank2, e2, cnt, coef, x1, mod3, gf):
    b, s, d = x1.shape
    t = b * s
    ne_total = u.shape[0]
    tm = min(512, s)
    eb = 2048
    ne = ne_total // eb
    per_b = s // tm
    nk16 = N_KEYS // BF16_ROWS
    kern = functools.partial(_peer_kernel, eb=eb, ne=ne, tm=tm)
    bblk = pl.BlockSpec((PEER_HEADS, nk16, BF16_ROWS, tm), lambda i, e: (0, 0, 0, i))
    fblk = pl.BlockSpec((PEER_HEADS, N_KEYS, tm), lambda i, e: (0, 0, i))
    xblk = pl.BlockSpec((None, tm, d), lambda i, e: (i // per_b, i % per_b, 0))
    return pl.pallas_call(
        kern,
        grid=(t // tm, ne),
        in_specs=[pl.BlockSpec((d, tm), lambda i, e: (0, i)),
                  pl.BlockSpec((eb, d), lambda i, e: (e, 0)),
                  pl.BlockSpec((d, eb), lambda i, e: (0, e)),
                  bblk, bblk, fblk, fblk,
                  xblk,
                  pl.BlockSpec((None, 1, d), lambda i, e: (i // per_b, 0, 5)),
                  pl.BlockSpec((1, d), lambda i, e: (0, 0))],
        out_specs=xblk,
        out_shape=jax.ShapeDtypeStruct((b, s, d), F32),
        scratch_shapes=[pltpu.VMEM((d, tm), F32), pltpu.VMEM((eb, tm), BF16),
                        pltpu.VMEM((eb, tm), BF16)],
        compiler_params=_cparams(("parallel", "arbitrary")),
        name="peer",
    )(h2t, u, vt, rank2, e2, cnt, coef, x1, mod3, gf)


def _sincos_2d(num_tokens, dim):
    rows = num_tokens // GRID_W
    row = jnp.repeat(jnp.arange(rows, dtype=F32), GRID_W)
    col = jnp.tile(jnp.arange(GRID_W, dtype=F32), rows)
    nf = dim // 4
    omega = 1.0 / (POS_BASE ** (jnp.arange(nf, dtype=F32) / nf))
    er = row[:, None] * omega[None]
    ec = col[:, None] * omega[None]
    return jnp.concatenate([jnp.sin(er), jnp.cos(er), jnp.sin(ec), jnp.cos(ec)], axis=-1)


def _dft_tables(n):
    q = 64 if n % 64 == 0 and n > 64 else n
    p = n // q
    k = jnp.arange(n, dtype=jnp.int32)[:, None]
    ang1 = ((k * jnp.arange(p, dtype=jnp.int32)[None, :]) % p).astype(F32) * (2.0 * math.pi / p)
    ang2 = ((k * jnp.arange(q, dtype=jnp.int32)[None, :]) % n).astype(F32) * (2.0 * math.pi / n)
    c1, s1 = jnp.cos(ang1)[:, :, None], jnp.sin(ang1)[:, :, None]
    c2, s2 = jnp.cos(ang2)[:, None, :], jnp.sin(ang2)[:, None, :]
    return (c1 * c2 - s1 * s2).reshape(n, n), (s1 * c2 + c1 * s2).reshape(n, n)


def kernel(x, c, ctx, c_ctx, w_mod, b_mod, norm1_g, norm2_g, w_in, w_fourier, conv_w, conv_b,
           lru_wa, lru_ba, lru_wx, lru_bx, lru_lambda, w_lru_out, w_out, b_out,
           peer_wq, peer_keys, peer_u, peer_v, final_g):
    b, s, d = x.shape
    ctx_len = ctx.shape[1]
    f = FOURIER_GROUPS * FOURIER_GROUP_DIM
    r = LRU_HEADS * LRU_HEAD_DIM
    assert w_mod.shape[0] == 1, "single-layer block"
    assert w_in.shape[2] == f + 2 * r + 2 * d

    pos = _sincos_2d(s, d)
    ct, st = _dft_tables(s)
    cc, sc = _dft_tables(FOURIER_GROUP_DIM)
    scale = 1.0 / math.sqrt(s * FOURIER_GROUP_DIM)
    eye = jnp.eye(FOURIER_GROUPS, dtype=F32)
    wc = jnp.concatenate([jnp.kron(eye, cc), jnp.kron(eye, sc)], axis=1) * scale

    rp = -(-(b + 1) // SUBLANES) * SUBLANES
    cond = jnp.concatenate([c, c_ctx[None], jnp.zeros((rp - b - 1, d), F32)], axis=0)
    mod = _adaln(cond, w_mod[0], b_mod)
    mod3 = mod.reshape(rp, 1, 6 * d)

    w_in_b = w_in[0].astype(BF16)
    g1 = norm1_g
    g2 = norm2_g
    wa = (0.5 * lru_wa[0]).astype(BF16)
    wx = (0.5 * lru_wx[0]).astype(BF16)
    ba, bx, lam = 0.5 * lru_ba[0], 0.5 * lru_bx[0], lru_lambda[0]

    def scans(zx, h0f, h0b):
        hf = _scan(zx, conv_w[0], conv_b, wa[0], ba[0:1], wx[0], bx[0:1], lam[0:1], h0f, False)
        hb = _scan(zx, conv_w[0], conv_b, wa[1], ba[1:2], wx[1], bx[1:2], lam[1:2], h0b, True)
        return hf, hb

    zx_c = _ctx_proj(ctx.reshape(b * ctx_len, d), mod3, b, g1, w_in_b[:, f:f + r])
    zeros = jnp.zeros((b, 1, r), F32)
    hf_c, hb_c = scans(zx_c.reshape(b, ctx_len, r), zeros, zeros)
    htf = hf_c[:, ctx_len - 1:ctx_len, :]
    htb = hb_c[:, 0:1, :]

    ab, zx, gy, sga, sgb = _proj_in(x, pos, mod3, g1, w_in_b, wc.astype(BF16), f, r)
    hf, hb = scans(zx, htf, htb)
    x1, h2t = _mix(x, pos, mod3, g2, hf, hb, gy, sga, sgb, ab,
                   ct.astype(BF16), st.astype(BF16),
                   w_fourier[0].astype(BF16), w_lru_out[0].astype(BF16),
                   w_out[0].astype(BF16), b_out, f)

    keys = peer_keys[0].reshape(2 * PEER_HEADS, N_KEYS, PEER_HALF_DIM)
    rank2, e2, cnt, coef = _route(h2t, _keyproj(keys, peer_wq[0]))
    u = peer_u[0].astype(BF16)
    vt = peer_v[0].T.astype(BF16)
    return _peer(h2t, u, vt, rank2, e2, cnt, coef, x1, mod3, final_g[None])
```
